```python
import jax, jax.numpy as jnp
from jax import lax
import numpy as np

D_MODEL = 1024
BATCH = 8
SEQ = 2048
DEPTH = 2

HEAD_DIM = 64
N_BRANCHES = 4
BRANCH_WIDTH = D_MODEL // N_BRANCHES
BRANCH_HEADS = BRANCH_WIDTH // HEAD_DIM
MOBA_BLOCK = 256
MOBA_TOPK = 3
MOBA_QCHUNK = 32
GLA_KEY_DIM = HEAD_DIM // 2
GLA_GATE_RANK = 16
GLA_GATE_TEMP = 16.0
LINEAR_CHUNK = 64
RET_DECAY_BASE = 5.0
SWA_KV_HEADS = 2
WINDOW = 128
N_ALIBI_HEADS = 2 * BRANCH_HEADS
N_EXPERTS = 32
TOP_K = 4
D_EXPERT = D_MODEL
SWIGLU_ALPHA = 1.702
SWIGLU_LIMIT = 7.0
NORM_EPS = 1e-5

GLA_QK_WIDTH = BRANCH_HEADS * GLA_KEY_DIM
SWA_KV_WIDTH = SWA_KV_HEADS * HEAD_DIM
IN_SIZES = (
    BRANCH_WIDTH, BRANCH_WIDTH, BRANCH_WIDTH,
    GLA_QK_WIDTH, GLA_QK_WIDTH, BRANCH_WIDTH, GLA_GATE_RANK, BRANCH_WIDTH,
    BRANCH_WIDTH, BRANCH_WIDTH, BRANCH_WIDTH, BRANCH_WIDTH,
    BRANCH_WIDTH, SWA_KV_WIDTH, SWA_KV_WIDTH,
    N_BRANCHES * D_MODEL,
)
D_IN = sum(IN_SIZES)

kernel_name = "hybrid_moba_gla_retnet_swa_moe_block"


def rms_norm(x, g):
    xf = x.astype(jnp.float32)
    y = xf * lax.rsqrt(jnp.mean(xf * xf, axis=-1, keepdims=True) + NORM_EPS)
    return (y * g.astype(jnp.float32)).astype(x.dtype)


def split_cols(z, sizes):
    offs = [int(o) for o in np.cumsum(sizes)[:-1]]
    return jnp.split(z, offs, axis=-1)


def to_heads(t, n_heads):
    b, s, _ = t.shape
    return t.reshape(b, s, n_heads, -1).transpose(0, 2, 1, 3)


def from_heads(o):
    b, h, s, d = o.shape
    return o.transpose(0, 2, 1, 3).reshape(b, s, h * d)


def head_rms_norm(o, g):
    h, d = o.shape[1], o.shape[3]
    of = o.astype(jnp.float32)
    y = of * lax.rsqrt(jnp.mean(of * of, axis=-1, keepdims=True) + NORM_EPS)
    return y * g.astype(jnp.float32).reshape(1, h, 1, d)


def head_group_norm(o, g):
    h, d = o.shape[1], o.shape[3]
    of = o.astype(jnp.float32)
    mu = jnp.mean(of, axis=-1, keepdims=True)
    var = jnp.mean(jnp.square(of - mu), axis=-1, keepdims=True)
    return (of - mu) * lax.rsqrt(var + NORM_EPS) * g.astype(jnp.float32).reshape(1, h, 1, d)


def moba_attention(q, k, v, slopes):
    B, H, S, dh = q.shape
    n_blocks = -(-S // MOBA_BLOCK)
    s_pad = n_blocks * MOBA_BLOCK
    pad = ((0, 0), (0, 0), (0, s_pad - S), (0, 0))
    q, k, v = jnp.pad(q, pad), jnp.pad(k, pad), jnp.pad(v, pad)
    kb = k.reshape(B, H, n_blocks, MOBA_BLOCK, dh)
    vb = v.reshape(B, H, n_blocks, MOBA_BLOCK, dh)
    k_mean = jnp.mean(kb.astype(jnp.float32), axis=3)
    k_sel = min(MOBA_TOPK, n_blocks)
    scale = dh ** -0.5
    b_idx = jnp.arange(B)[:, None, None, None]
    h_idx = jnp.arange(H)[None, :, None, None]
    sl = slopes.astype(jnp.float32)
    blk_offs = jnp.arange(MOBA_BLOCK)

    def chunk_fn(ci):
        start = ci * MOBA_QCHUNK
        q_c = lax.dynamic_slice_in_dim(q, start, MOBA_QCHUNK, axis=2)
        t = (start + jnp.arange(MOBA_QCHUNK)).astype(jnp.float32)
        blk = start // MOBA_BLOCK
        gate = jnp.einsum('bhqd,bhnd->bhqn', q_c.astype(jnp.float32), k_mean)
        gate = jnp.where(jnp.arange(n_blocks) < blk, gate, -jnp.inf)
        _, idx = lax.top_k(gate, k_sel)
        valid = idx < blk
        k_g = kb[b_idx, h_idx, idx]
        v_g = vb[b_idx, h_idx, idx]
        s_g = jnp.einsum('bhqd,bhqjsd->bhqjs', q_c, k_g).astype(jnp.float32) * scale
        pos_g = (idx[..., None] * MOBA_BLOCK + blk_offs).astype(jnp.float32)
        dist_g = t[None, None, :, None, None] - pos_g
        s_g = jnp.where(valid[..., None], s_g - sl[None, :, None, None, None] * dist_g, -jnp.inf)
        k_o = lax.dynamic_index_in_dim(kb, blk, axis=2, keepdims=False)
        v_o = lax.dynamic_index_in_dim(vb, blk, axis=2, keepdims=False)
        s_o = jnp.einsum('bhqd,bhsd->bhqs', q_c, k_o).astype(jnp.float32) * scale
        dist_o = t[:, None] - (blk * MOBA_BLOCK + blk_offs).astype(jnp.float32)[None, :]
        s_o = jnp.where(dist_o >= 0, s_o - sl[None, :, None, None] * dist_o, -jnp.inf)
        logits = jnp.concatenate([s_g.reshape(B, H, MOBA_QCHUNK, k_sel * MOBA_BLOCK), s_o], axis=-1)
        p = jax.nn.softmax(logits, axis=-1)
        p_g = p[..., :k_sel * MOBA_BLOCK].reshape(B, H, MOBA_QCHUNK, k_sel, MOBA_BLOCK).astype(v.dtype)
        p_o = p[..., k_sel * MOBA_BLOCK:].astype(v.dtype)
        return (jnp.einsum('bhqjs,bhqjsd->bhqd', p_g, v_g)
                + jnp.einsum('bhqs,bhsd->bhqd', p_o, v_o))

    out = lax.map(chunk_fn, jnp.arange(s_pad // MOBA_QCHUNK))
    out = out.transpose(1, 2, 0, 3, 4).reshape(B, H, s_pad, dh)
    return out[:, :, :S]


def chunked_decay_linear_attention(q, k, v, log_a):
    B, H, S, dk = q.shape
    dv = v.shape[-1]
    L = LINEAR_CHUNK
    N = S // L
    q, k, v, log_a = (t.astype(jnp.float32).reshape(B, H, N, L, t.shape[-1]) for t in (q, k, v, log_a))
    b = jnp.cumsum(log_a, axis=3)
    b_end = b[:, :, :, -1:, :]
    q_dec = q * jnp.exp(b)
    k_in = k * jnp.exp(-b)
    k_end = k * jnp.exp(b_end - b)
    causal = jnp.tril(jnp.ones((L, L), dtype=bool))
    a = jnp.where(causal, jnp.einsum('bhnld,bhnmd->bhnlm', q_dec, k_in), 0.0)
    o_intra = jnp.einsum('bhnlm,bhnmv->bhnlv', a, v)
    kv = jnp.einsum('bhnld,bhnlv->nbhdv', k_end, v)
    decay = jnp.exp(jnp.moveaxis(b_end[:, :, :, 0, :], 2, 0))

    def step(state, inp):
        kv_n, d_n = inp
        return d_n[..., None] * state + kv_n, state

    _, s_prev = lax.scan(step, jnp.zeros((B, H, dk, dv), jnp.float32), (kv, decay))
    o_inter = jnp.einsum('bhnld,nbhdv->bhnlv', q_dec, s_prev)
    return (o_intra + o_inter).reshape(B, H, S, dv)


def sliding_window_attention(q, k, v, sinks, slopes):
    B, Hq, S, dh = q.shape
    Hkv = k.shape[1]
    G = Hq // Hkv
    W = WINDOW
    nq = S // W
    qb = q.reshape(B, Hkv, G, nq, W, dh)

    def band(t):
        tp = jnp.pad(t, ((0, 0), (0, 0), (W, 0), (0, 0))).reshape(B, Hkv, nq + 1, W, dh)
        return jnp.concatenate([tp[:, :, :-1], tp[:, :, 1:]], axis=3)

    kb, vb = band(k), band(v)
    s = jnp.einsum('bkgnqd,bknsd->bkgnqs', qb, kb).astype(jnp.float32) * dh ** -0.5
    dist = (W + jnp.arange(W)[:, None] - jnp.arange(2 * W)[None, :])
    key_pos = jnp.arange(nq)[:, None] * W - W + jnp.arange(2 * W)[None, :]
    allowed = (dist >= 0)[None] & (dist < W)[None] & (key_pos >= 0)[:, None, :]
    sl = slopes.astype(jnp.float32).reshape(Hkv, G)[None, :, :, None, None, None]
    s = jnp.where(allowed, s - sl * dist.astype(jnp.float32), -jnp.inf)
    sink = jnp.broadcast_to(sinks.astype(jnp.float32).reshape(Hkv, G)[None, :, :, None, None, None],
                            s.shape[:-1] + (1,))
    p = jax.nn.softmax(jnp.concatenate([s, sink], axis=-1), axis=-1)[..., :-1]
    o = jnp.einsum('bkgnqs,bknsd->bkgnqd', p.astype(v.dtype), vb)
    return o.reshape(B, Hq, S, dh)


def moe_ffn(h, w_router, b_router, w_gate_up, b_gate_up, w_down, b_down):
    logits = (h @ w_router + b_router).astype(jnp.float32)
    top_vals, top_idx = lax.top_k(logits, TOP_K)
    weights = jax.nn.softmax(top_vals, axis=-1)
    combine = jnp.einsum('tk,tke->te', weights,
                         jax.nn.one_hot(top_idx, N_EXPERTS, dtype=jnp.float32)).astype(h.dtype)
    y = jnp.zeros_like(h)
    for e in range(N_EXPERTS):
        gu = h @ w_gate_up[e] + b_gate_up[e]
        x_glu, x_lin = jnp.split(gu, 2, axis=-1)
        x_glu = jnp.minimum(x_glu, SWIGLU_LIMIT)
        x_lin = jnp.clip(x_lin, -SWIGLU_LIMIT, SWIGLU_LIMIT)
        act = x_glu * jax.nn.sigmoid(SWIGLU_ALPHA * x_glu) * (x_lin + 1.0)
        y = y + combine[:, e:e + 1] * (act @ w_down[e] + b_down[e])
    return y


def setup_inputs(seed: int = 0) -> dict:
    key = jax.random.key(seed)
    ks = jax.random.split(key, 22)
    f32 = jnp.float32
    nrm = lambda k, shape, s: jax.random.normal(k, shape, f32) * s
    return {
        "x": nrm(ks[0], (BATCH, SEQ, D_MODEL), 1.0),
        "c": nrm(ks[1], (BATCH, D_MODEL), 1.0),
        "w_ada": nrm(ks[2], (DEPTH, D_MODEL, 6 * D_MODEL), 0.5 * D_MODEL ** -0.5),
        "b_ada": nrm(ks[3], (DEPTH, 6 * D_MODEL), 0.01),
        "g_norm_mix": 1.0 + nrm(ks[4], (DEPTH, D_MODEL), 0.01),
        "w_in": nrm(ks[5], (DEPTH, D_MODEL, D_IN), D_MODEL ** -0.5),
        "w_gla_gate": nrm(ks[6], (DEPTH, GLA_GATE_RANK, GLA_QK_WIDTH), GLA_GATE_RANK ** -0.5),
        "b_gla_gate": nrm(ks[7], (DEPTH, GLA_QK_WIDTH), 0.1),
        "g_gla_norm": 1.0 + nrm(ks[8], (DEPTH, BRANCH_WIDTH), 0.01),
        "g_ret_norm": 1.0 + nrm(ks[9], (DEPTH, BRANCH_WIDTH), 0.01),
        "attn_sinks": nrm(ks[10], (DEPTH, BRANCH_HEADS), 1.0),
        "w_branch": nrm(ks[11], (DEPTH, N_BRANCHES, BRANCH_WIDTH, D_MODEL), BRANCH_WIDTH ** -0.5),
        "w_out": nrm(ks[12], (DEPTH, D_MODEL, D_MODEL), D_MODEL ** -0.5),
        "g_norm_ffn": 1.0 + nrm(ks[13], (DEPTH, D_MODEL), 0.01),
        "w_router": nrm(ks[14], (DEPTH, D_MODEL, N_EXPERTS), D_MODEL ** -0.5),
        "b_router": nrm(ks[15], (DEPTH, N_EXPERTS), 0.01),
        "w_gate_up": nrm(ks[16], (DEPTH, N_EXPERTS, D_MODEL, 2 * D_EXPERT), D_MODEL ** -0.5),
        "b_gate_up": nrm(ks[17], (DEPTH, N_EXPERTS, 2 * D_EXPERT), 0.01),
        "w_down": nrm(ks[18], (DEPTH, N_EXPERTS, D_EXPERT, D_MODEL), D_EXPERT ** -0.5),
        "b_down": nrm(ks[19], (DEPTH, N_EXPERTS, D_MODEL), 0.01),
        "g_final": 1.0 + nrm(ks[20], (D_MODEL,), 0.01),
    }


def reference(x, c, w_ada, b_ada, g_norm_mix, w_in, w_gla_gate, b_gla_gate, g_gla_norm, g_ret_norm,
              attn_sinks, w_branch, w_out, g_norm_ffn, w_router, b_router, w_gate_up, b_gate_up,
              w_down, b_down, g_final):
    B, S, D = x.shape
    H = BRANCH_HEADS
    slopes = 2.0 ** (-(jnp.arange(N_ALIBI_HEADS, dtype=jnp.float32) + 1.0) * (8.0 / N_ALIBI_HEADS))
    swa_slopes, moba_slopes = slopes[:H], slopes[H:]
    log_gamma = jnp.log(1.0 - 2.0 ** (-RET_DECAY_BASE - jnp.arange(H, dtype=jnp.float32)))
    c_act = jax.nn.silu(c)

    for l in range(DEPTH):
        mod = c_act @ w_ada[l] + b_ada[l]
        shift1, scale1, gate1, shift2, scale2, gate2 = (m[:, None, :] for m in jnp.split(mod, 6, axis=-1))

        h = rms_norm(x, g_norm_mix[l]) * (1.0 + scale1) + shift1
        z = h @ w_in[l]
        (mq, mk, mv, gq, gk, gv, ga, gr, rq, rk, rv, rg, sq, sk, sv, merge_logits) = split_cols(z, IN_SIZES)

        y_moba = from_heads(moba_attention(to_heads(mq, H), to_heads(mk, H), to_heads(mv, H), moba_slopes))

        gate_logit = ga @ w_gla_gate[l] + b_gla_gate[l]
        log_a = jax.nn.log_sigmoid(gate_logit.astype(jnp.float32)) / GLA_GATE_TEMP
        o_gla = chunked_decay_linear_attention(to_heads(gq, H) * GLA_KEY_DIM ** -0.5, to_heads(gk, H),
                                               to_heads(gv, H), to_heads(log_a, H))
        y_gla = (from_heads(head_rms_norm(o_gla, g_gla_norm[l])) * jax.nn.silu(gr.astype(jnp.float32))).astype(x.dtype)

        ret_log_a = jnp.broadcast_to(log_gamma[None, :, None, None], (B, H, S, HEAD_DIM))
        o_ret = chunked_decay_linear_attention(to_heads(rq, H), to_heads(rk, H) * HEAD_DIM ** -0.5,
                                               to_heads(rv, H), ret_log_a)
        y_ret = (from_heads(head_group_norm(o_ret, g_ret_norm[l])) * jax.nn.silu(rg.astype(jnp.float32))).astype(x.dtype)

        y_swa = from_heads(sliding_window_attention(to_heads(sq, H), to_heads(sk, SWA_KV_HEADS),
                                                    to_heads(sv, SWA_KV_HEADS), attn_sinks[l], swa_slopes))

        ys = jnp.stack([y_moba, y_gla, y_ret, y_swa], axis=2)
        up = jnp.einsum('bsnc,ncd->bsnd', ys, w_branch[l])
        gates = jax.nn.sigmoid(merge_logits.reshape(B, S, N_BRANCHES, D))
        mixed = jnp.sum(gates * up, axis=2) @ w_out[l]
        x = x + gate1 * mixed

        h2 = rms_norm(x, g_norm_ffn[l]) * (1.0 + scale2) + shift2
        y_ffn = moe_ffn(h2.reshape(B * S, D), w_router[l], b_router[l], w_gate_up[l], b_gate_up[l],
                        w_down[l], b_down[l]).reshape(B, S, D)
        x = x + gate2 * y_ffn

    return rms_norm(x, g_final)
```

```python
import functools

import jax
import jax.numpy as jnp
import numpy as np
from jax import lax
from jax.experimental import pallas as pl
from jax.experimental.pallas import tpu as pltpu

F32 = jnp.float32
BF16 = jnp.bfloat16
I32 = jnp.int32
HI = lax.Precision.HIGHEST

HEAD_DIM = 64
N_BRANCHES = 4
BRANCH_HEADS = 4
BRANCH_WIDTH = HEAD_DIM * BRANCH_HEADS
MOBA_BLOCK = 256
MOBA_TOPK = 3
GLA_KEY_DIM = 32
GLA_GATE_RANK = 16
GLA_GATE_TEMP = 16.0
LINEAR_CHUNK = 64
RET_DECAY_BASE = 5.0
SWA_KV_HEADS = 2
WINDOW = 128
N_EXPERTS = 32
TOP_K = 4
SWIGLU_ALPHA = 1.702
SWIGLU_LIMIT = 7.0
NORM_EPS = 1e-5

LANES = 128
TM_IN = 512
TM_MERGE = 256
TM_EXPERT = 256
TM_COMBINE = 512
LIN_GROUP = 256
MOD_COLS = 1536
NEG_BIG = -1e30

_C_MOBA = (0, 768)
_C_GLA = (768, 1664)
_C_RET = (1664, 2688)
_C_SWA = (2688, 3456)
N_PACKED = 3456


def _dot(a, b, precision=None):
    return jnp.dot(a, b, preferred_element_type=F32, precision=precision)


def _dot_nt(a, b, precision=None):
    return lax.dot_general(a, b, (((1,), (1,)), ((), ())), preferred_element_type=F32, precision=precision)


def _dot_tn(a, b, precision=None):
    return lax.dot_general(a, b, (((0,), (0,)), ((), ())), preferred_element_type=F32, precision=precision)


def _norm_mod(x, g, scale, shift):
    ms = jnp.mean(x * x, axis=-1, keepdims=True)
    y = x * lax.rsqrt(ms + NORM_EPS) * g
    return y * (1.0 + scale) + shift


def _log_sigmoid(x):
    return jnp.minimum(x, 0.0) - jnp.log1p(jnp.exp(-jnp.abs(x)))


def _params(*sem):
    return pltpu.CompilerParams(dimension_semantics=tuple(sem))


def _mod_kernel(c_ref, w_ref, b_ref, o_ref):
    c = c_ref[...]
    o_ref[...] = _dot(jax.nn.silu(c), w_ref[...], HI) + b_ref[...]


def _modulation(c, w_ada, b_ada):
    depth, d, n = w_ada.shape
    b = c.shape[0]
    return pl.pallas_call(
        _mod_kernel,
        out_shape=jax.ShapeDtypeStruct((depth, b, n), F32),
        grid=(depth, n // MOD_COLS),
        in_specs=[
            pl.BlockSpec((b, d), lambda l, j: (0, 0)),
            pl.BlockSpec((None, d, MOD_COLS), lambda l, j: (l, 0, j)),
            pl.BlockSpec((None, 1, MOD_COLS), lambda l, j: (l, 0, j)),
        ],
        out_specs=pl.BlockSpec((None, b, MOD_COLS), lambda l, j: (l, 0, j)),
        compiler_params=_params("arbitrary", "arbitrary"),
        name="adaln_mod",
    )(c, w_ada, b_ada.reshape(depth, 1, n))


def _in_kernel(x_ref, sc_ref, sh_ref, g_ref, w_ref, wg_ref, bg_ref,
               moba_ref, glaf_ref, glav_ref, retf_ref, retv_ref, swa_ref, kmean_ref):
    h = _norm_mod(x_ref[...], g_ref[...], sc_ref[...], sh_ref[...]).astype(BF16)

    zm = _dot(h, w_ref[:, _C_MOBA[0]:_C_MOBA[1]])
    moba_ref[...] = zm.astype(BF16)
    for blk in range(TM_IN // MOBA_BLOCK):
        kb = zm[blk * MOBA_BLOCK:(blk + 1) * MOBA_BLOCK, 256:512]
        kmean_ref[blk:blk + 1, :] = jnp.mean(kb, axis=0, keepdims=True)

    zg = _dot(h, w_ref[:, _C_GLA[0]:_C_GLA[1]])
    gate_logit = _dot(zg[:, 512:640], wg_ref[...], HI) + bg_ref[...]
    glaf_ref[:, 0:128] = zg[:, 0:128] * (GLA_KEY_DIM ** -0.5)
    glaf_ref[:, 128:256] = zg[:, 128:256]
    glaf_ref[:, 256:384] = _log_sigmoid(gate_logit) / GLA_GATE_TEMP
    glaf_ref[:, 384:640] = zg[:, 640:896]
    glav_ref[...] = zg[:, 256:512].astype(BF16)

    zr = _dot(h, w_ref[:, _C_RET[0]:_C_RET[1]])
    retf_ref[:, 0:512] = zr[:, 0:512]
    retf_ref[:, 512:768] = zr[:, 768:1024]
    retv_ref[...] = zr[:, 512:768].astype(BF16)

    swa_ref[...] = _dot(h, w_ref[:, _C_SWA[0]:_C_SWA[1]]).astype(BF16)


def _input_projection(x2, scale, shift, g, w_packed, wg, bg, seq):
    t, d = x2.shape
    tiles_per_seq = seq // TM_IN
    nblk = TM_IN // MOBA_BLOCK
    row = lambda i: (i, 0)
    per_batch = lambda i: (i // tiles_per_seq, 0, 0)
    const2 = lambda i: (0, 0)
    outs = pl.pallas_call(
        _in_kernel,
        out_shape=(
            jax.ShapeDtypeStruct((t, 768), BF16),
            jax.ShapeDtypeStruct((t, 640), F32),
            jax.ShapeDtypeStruct((t, 256), BF16),
            jax.ShapeDtypeStruct((t, 768), F32),
            jax.ShapeDtypeStruct((t, 256), BF16),
            jax.ShapeDtypeStruct((t, 768), BF16),
            jax.ShapeDtypeStruct((t // TM_IN, nblk, 256), F32),
        ),
        grid=(t // TM_IN,),
        in_specs=[
            pl.BlockSpec((TM_IN, d), row),
            pl.BlockSpec((None, 1, d), per_batch),
            pl.BlockSpec((None, 1, d), per_batch),
            pl.BlockSpec((1, d), const2),
            pl.BlockSpec((d, N_PACKED), const2),
            pl.BlockSpec((LANES, LANES), const2),
            pl.BlockSpec((1, LANES), const2),
        ],
        out_specs=(
            pl.BlockSpec((TM_IN, 768), row),
            pl.BlockSpec((TM_IN, 640), row),
            pl.BlockSpec((TM_IN, 256), row),
            pl.BlockSpec((TM_IN, 768), row),
            pl.BlockSpec((TM_IN, 256), row),
            pl.BlockSpec((TM_IN, 768), row),
            pl.BlockSpec((None, nblk, 256), lambda i: (i, 0, 0)),
        ),
        compiler_params=_params("arbitrary"),
        name="input_proj",
    )(x2, scale, shift, g, w_packed, wg, bg)
    return outs


def _moba_kernel(q_ref, k_ref, v_ref, km_ref, o_ref, m_sc, l_sc, acc_sc, *, slopes):
    i = pl.program_id(1)
    nb = MOBA_BLOCK
    n_blocks = km_ref.shape[0]
    q = q_ref[...]
    lane = lax.broadcasted_iota(I32, (1, BRANCH_WIDTH), 1)
    head_of_lane = lane // HEAD_DIM

    rows = lax.broadcasted_iota(I32, (LANES, BRANCH_WIDTH), 0)
    lanes_full = lax.broadcasted_iota(I32, (LANES, BRANCH_WIDTH), 1)
    km = km_ref[...]
    km_t = jnp.concatenate([km] * (LANES // n_blocks), axis=0)
    km_heads = jnp.where((rows // n_blocks) == (lanes_full // HEAD_DIM), km_t, 0.0)
    gate_t = _dot_nt(km_heads, q.astype(F32), HI)

    jrow = lax.broadcasted_iota(I32, (n_blocks, nb), 0)
    valid = jrow < i
    sel_rows = []
    for h in range(BRANCH_HEADS):
        g = jnp.where(valid, gate_t[h * n_blocks:(h + 1) * n_blocks, :], -jnp.inf)
        cnt = jnp.zeros((n_blocks, nb), F32)
        for jp in range(n_blocks):
            gj = g[jp:jp + 1, :]
            beats = (gj > g) | ((gj == g) & (jp < jrow))
            cnt = cnt + beats.astype(F32)
        sel_rows.append(((cnt < MOBA_TOPK) & valid).astype(F32))
    sel_rows.append(jnp.zeros((LANES - BRANCH_HEADS * n_blocks, nb), F32))
    sel_t = jnp.concatenate(sel_rows, axis=0).T

    r_io = lax.broadcasted_iota(I32, (nb, nb), 0)
    c_io = lax.broadcasted_iota(I32, (nb, nb), 1)
    rel = (r_io - c_io).astype(F32)
    lane128 = lax.broadcasted_iota(I32, (nb, LANES), 1)

    m_sc[...] = jnp.full(m_sc.shape, -jnp.inf, F32)
    l_sc[...] = jnp.zeros(l_sc.shape, F32)
    acc_sc[...] = jnp.zeros(acc_sc.shape, F32)

    def process(j, own):
        start = pl.multiple_of(j * nb, nb)
        kj = k_ref[pl.ds(start, nb), :]
        vj = v_ref[pl.ds(start, nb), :]
        dist = rel + ((i - j) * nb).astype(F32)
        for h in range(BRANCH_HEADS):
            hm = head_of_lane == h
            s = _dot_nt(jnp.where(hm, q, jnp.zeros_like(q)), kj) - slopes[h] * dist
            if own:
                mask = rel >= 0.0
            else:
                col = jnp.sum(jnp.where(lane128 == h * n_blocks + j, sel_t, 0.0), axis=1, keepdims=True)
                mask = col > 0.5
            s = jnp.where(mask, s, -jnp.inf)
            m_old = m_sc[h]
            m_new = jnp.maximum(m_old, jnp.max(s, axis=1, keepdims=True))
            alpha = jnp.exp(m_old - m_new)
            p = jnp.exp(s - m_new)
            l_sc[h] = alpha * l_sc[h] + jnp.sum(p, axis=1, keepdims=True)
            m_sc[h] = m_new
            pv = _dot(p.astype(BF16), jnp.where(hm, vj, jnp.zeros_like(vj)))
            acc_sc[...] = acc_sc[...] * jnp.where(hm, alpha, 1.0) + pv

    process(i, True)

    def body(j, carry):
        process(j, False)
        return carry

    lax.fori_loop(0, i, body, 0)

    inv = jnp.zeros((nb, BRANCH_WIDTH), F32)
    for h in range(BRANCH_HEADS):
        inv = jnp.where(head_of_lane == h, 1.0 / l_sc[h], inv)
    o_ref[...] = (acc_sc[...] * inv).astype(BF16)


def _moba(qkv, kmean, batch, seq, slopes):
    t = qkv.shape[0]
    nq = seq // MOBA_BLOCK
    kern = functools.partial(_moba_kernel, slopes=slopes)
    return pl.pallas_call(
        kern,
        out_shape=jax.ShapeDtypeStruct((t, BRANCH_WIDTH), BF16),
        grid=(batch, nq),
        in_specs=[
            pl.BlockSpec((MOBA_BLOCK, BRANCH_WIDTH), lambda b, i: (b * nq + i, 0)),
            pl.BlockSpec((seq, BRANCH_WIDTH), lambda b, i: (b, 1)),
            pl.BlockSpec((seq, BRANCH_WIDTH), lambda b, i: (b, 2)),
            pl.BlockSpec((None, nq, BRANCH_WIDTH), lambda b, i: (b, 0, 0)),
        ],
        out_specs=pl.BlockSpec((MOBA_BLOCK, BRANCH_WIDTH), lambda b, i: (b * nq + i, 0)),
        scratch_shapes=[
            pltpu.VMEM((BRANCH_HEADS, MOBA_BLOCK, 1), F32),
            pltpu.VMEM((BRANCH_HEADS, MOBA_BLOCK, 1), F32),
            pltpu.VMEM((MOBA_BLOCK, BRANCH_WIDTH), F32),
        ],
        compiler_params=_params("arbitrary", "arbitrary"),
        name="moba_attn",
    )(qkv, qkv, qkv, kmean)


def _swa_kernel(q_ref, kp_ref, kc_ref, vp_ref, vc_ref, sink_ref, o_ref, *, slopes):
    n = pl.program_id(1)
    w = WINDOW
    q = q_ref[...]
    kp, kc, vp, vc = kp_ref[...], kc_ref[...], vp_ref[...], vc_ref[...]
    head_of_lane = lax.broadcasted_iota(I32, (1, BRANCH_WIDTH), 1) // HEAD_DIM
    r_io = lax.broadcasted_iota(I32, (w, w), 0)
    c_io = lax.broadcasted_iota(I32, (w, w), 1)
    rel = (r_io - c_io).astype(F32)
    allow_cur = rel >= 0.0
    allow_prev = (rel < 0.0) & (n > 0)
    out = jnp.zeros((w, BRANCH_WIDTH), F32)
    zq = jnp.zeros_like(q)
    zv = jnp.zeros_like(vp)
    for h in range(BRANCH_HEADS):
        hm = head_of_lane == h
        qh = jnp.where(hm, q, zq)
        sp = jnp.where(allow_prev, _dot_nt(qh, kp) - slopes[h] * (rel + float(w)), -jnp.inf)
        sc = jnp.where(allow_cur, _dot_nt(qh, kc) - slopes[h] * rel, -jnp.inf)
        sink = sink_ref[0:1, h:h + 1]
        m = jnp.maximum(jnp.maximum(jnp.max(sp, axis=1, keepdims=True), jnp.max(sc, axis=1, keepdims=True)), sink)
        pp = jnp.exp(sp - m)
        pc = jnp.exp(sc - m)
        l = jnp.sum(pp, axis=1, keepdims=True) + jnp.sum(pc, axis=1, keepdims=True) + jnp.exp(sink - m)
        o = _dot(pp.astype(BF16), jnp.where(hm, vp, zv)) + _dot(pc.astype(BF16), jnp.where(hm, vc, zv))
        out = out + o * (1.0 / l)
    o_ref[...] = out.astype(BF16)


def _swa(qkv, sinks_row, batch, seq, slopes):
    t = qkv.shape[0]
    nq = seq // WINDOW
    kern = functools.partial(_swa_kernel, slopes=slopes)
    cur = lambda col: (lambda b, n: (b * nq + n, col))
    prev = lambda col: (lambda b, n: (b * nq + jnp.maximum(n - 1, 0), col))
    blk = (WINDOW, BRANCH_WIDTH)
    return pl.pallas_call(
        kern,
        out_shape=jax.ShapeDtypeStruct((t, BRANCH_WIDTH), BF16),
        grid=(batch, nq),
        in_specs=[
            pl.BlockSpec(blk, cur(0)),
            pl.BlockSpec(blk, prev(1)),
            pl.BlockSpec(blk, cur(1)),
            pl.BlockSpec(blk, prev(2)),
            pl.BlockSpec(blk, cur(2)),
            pl.BlockSpec((1, LANES), lambda b, n: (0, 0)),
        ],
        out_specs=pl.BlockSpec(blk, cur(0)),
        compiler_params=_params("arbitrary", "arbitrary"),
        name="swa_attn",
    )(qkv, qkv, qkv, qkv, qkv, sinks_row)


def _lin_kernel(f_ref, v_ref, g_ref, lg_ref, o_ref, st_sc, *, dk, gated, qoff, koff, laoff, roff, norm):
    width = BRANCH_HEADS * dk
    grp = LIN_GROUP
    ch = LINEAR_CHUNK
    seq = f_ref.shape[0]

    r_io = lax.broadcasted_iota(I32, (grp, grp), 0)
    c_io = lax.broadcasted_iota(I32, (grp, grp), 1)
    same = (r_io // ch) == (c_io // ch)
    ltri = (same & (c_io <= r_io)).astype(F32)
    lfull = same.astype(F32)
    head_avg = ((r_io // HEAD_DIM) == (c_io // HEAD_DIM)).astype(F32) * (1.0 / HEAD_DIM)
    lane_k = lax.broadcasted_iota(I32, (1, width), 1) // dk
    lane_v = lax.broadcasted_iota(I32, (1, BRANCH_WIDTH), 1) // HEAD_DIM
    bd_mask = (lax.broadcasted_iota(I32, (BRANCH_WIDTH, width), 0) // HEAD_DIM
               == lax.broadcasted_iota(I32, (BRANCH_WIDTH, width), 1) // dk)
    gain = g_ref[...]

    st_sc[...] = jnp.zeros(st_sc.shape, F32)

    def group(gi, carry):
        r0 = pl.multiple_of(gi * grp, grp)
        rows = pl.ds(r0, grp)
        q = f_ref[rows, qoff:qoff + width]
        k = f_ref[rows, koff:koff + width]
        if gated:
            la = f_ref[rows, laoff:laoff + width]
        else:
            la = jnp.broadcast_to(lg_ref[...], (grp, width))
        b = _dot(ltri, la, HI)
        be = _dot(lfull, la, HI)
        qd = (q * jnp.exp(b)).astype(BF16)
        kin = (k * jnp.exp(-b)).astype(BF16)
        kend = (k * jnp.exp(be - b)).astype(BF16)
        dec = jnp.exp(be)
        v = v_ref[rows, :]
        zq = jnp.zeros_like(qd)
        zv = jnp.zeros_like(v)

        o = jnp.zeros((grp, BRANCH_WIDTH), F32)
        for h in range(BRANCH_HEADS):
            a = _dot_nt(jnp.where(lane_k == h, qd, zq), kin) * ltri
            o = o + _dot(a.astype(BF16), jnp.where(lane_v == h, v, zv))

        st = st_sc[...]
        parts = []
        for c in range(grp // ch):
            sl = slice(c * ch, (c + 1) * ch)
            parts.append(_dot_nt(qd[sl], st.astype(BF16)))
            kv_t = _dot_tn(v[sl], kend[sl])
            st = st * dec[c * ch:c * ch + 1, :] + jnp.where(bd_mask, kv_t, 0.0)
        st_sc[...] = st
        o = o + jnp.concatenate(parts, axis=0)

        if norm == "rms":
            ms = _dot(o * o, head_avg, HI)
            y = o * lax.rsqrt(ms + NORM_EPS) * gain
        else:
            mu = _dot(o, head_avg, HI)
            xc = o - mu
            var = _dot(xc * xc, head_avg, HI)
            y = xc * lax.rsqrt(var + NORM_EPS) * gain
        r = f_ref[rows, roff:roff + BRANCH_WIDTH]
        o_ref[rows, :] = (y * jax.nn.silu(r)).astype(BF16)
        return carry

    lax.fori_loop(0, seq // grp, group, 0)


def _linear_attention(feat, v, gain, log_gamma_row, batch, seq, *, dk, gated, qoff, koff, laoff, roff, norm, name):
    t, fw = feat.shape
    width = BRANCH_HEADS * dk
    kern = functools.partial(_lin_kernel, dk=dk, gated=gated, qoff=qoff, koff=koff, laoff=laoff, roff=roff, norm=norm)
    return pl.pallas_call(
        kern,
        out_shape=jax.ShapeDtypeStruct((t, BRANCH_WIDTH), BF16),
        grid=(batch,),
        in_specs=[
            pl.BlockSpec((seq, fw), lambda b: (b, 0)),
            pl.BlockSpec((seq, BRANCH_WIDTH), lambda b: (b, 0)),
            pl.BlockSpec((1, BRANCH_WIDTH), lambda b: (0, 0)),
            pl.BlockSpec((1, width), lambda b: (0, 0)),
        ],
        out_specs=pl.BlockSpec((seq, BRANCH_WIDTH), lambda b: (b, 0)),
        scratch_shapes=[pltpu.VMEM((BRANCH_WIDTH, width), F32)],
        compiler_params=_params("arbitrary"),
        name=name,
    )(feat, v, gain, log_gamma_row)


def _merge_kernel(x_ref, sc1_ref, sh1_ref, gt1_ref, gnm_ref, ym_ref, yg_ref, yr_ref, ys_ref,
                  wm_ref, wb_ref, wo_ref, gnf_ref, sc2_ref, sh2_ref, wr_ref, br_ref,
                  x1_ref, h2_ref, ti_ref, tw_ref):
    x = x_ref[...]
    d = x.shape[1]
    hb = _norm_mod(x, gnm_ref[...], sc1_ref[...], sh1_ref[...]).astype(BF16)
    mixed = jnp.zeros(x.shape, F32)
    for n, y_ref in enumerate((ym_ref, yg_ref, yr_ref, ys_ref)):
        gate = jax.nn.sigmoid(_dot(hb, wm_ref[:, n * d:(n + 1) * d]))
        mixed = mixed + gate * _dot(y_ref[...], wb_ref[n])
    x1 = x + gt1_ref[...] * _dot(mixed.astype(BF16), wo_ref[...])
    x1_ref[...] = x1

    h2 = _norm_mod(x1, gnf_ref[...], sc2_ref[...], sh2_ref[...])
    h2_ref[...] = h2
    logits = _dot(h2, wr_ref[...], HI) + br_ref[...]
    lane = lax.broadcasted_iota(I32, logits.shape, 1)
    vals, idxs = [], []
    cur = logits
    for _ in range(TOP_K):
        m = jnp.max(cur, axis=1, keepdims=True)
        idx = jnp.min(jnp.where(cur == m, lane, LANES), axis=1, keepdims=True)
        vals.append(m)
        idxs.append(idx)
        cur = jnp.where(lane == idx, -jnp.inf, cur)
    es = [jnp.exp(v - vals[0]) for v in vals]
    tot = es[0] + es[1] + es[2] + es[3]
    ti = jnp.zeros(logits.shape, I32)
    tw = jnp.zeros(logits.shape, F32)
    for k in range(TOP_K):
        ti = jnp.where(lane == k, idxs[k], ti)
        tw = jnp.where(lane == k, es[k] / tot, tw)
    ti_ref[...] = ti
    tw_ref[...] = tw


def _merge(x2, sc1, sh1, gt1, gnm, ys, wm, wb, wo, gnf, sc2, sh2, wr, br, seq):
    t, d = x2.shape
    tiles_per_seq = seq // TM_MERGE
    row = lambda i: (i, 0)
    per_batch = lambda i: (i // tiles_per_seq, 0, 0)
    c2 = lambda i: (0, 0)
    c3 = lambda i: (0, 0, 0)
    vec = pl.BlockSpec((None, 1, d), per_batch)
    ytile = pl.BlockSpec((TM_MERGE, BRANCH_WIDTH), row)
    return pl.pallas_call(
        _merge_kernel,
        out_shape=(
            jax.ShapeDtypeStruct((t, d), F32),
            jax.ShapeDtypeStruct((t, d), F32),
            jax.ShapeDtypeStruct((t, LANES), I32),
            jax.ShapeDtypeStruct((t, LANES), F32),
        ),
        grid=(t // TM_MERGE,),
        in_specs=[
            pl.BlockSpec((TM_MERGE, d), row), vec, vec, vec,
            pl.BlockSpec((1, d), c2),
            ytile, ytile, ytile, ytile,
            pl.BlockSpec((d, N_BRANCHES * d), c2),
            pl.BlockSpec((N_BRANCHES, BRANCH_WIDTH, d), c3),
            pl.BlockSpec((d, d), c2),
            pl.BlockSpec((1, d), c2), vec, vec,
            pl.BlockSpec((d, LANES), c2),
            pl.BlockSpec((1, LANES), c2),
        ],
        out_specs=(
            pl.BlockSpec((TM_MERGE, d), row),
            pl.BlockSpec((TM_MERGE, d), row),
            pl.BlockSpec((TM_MERGE, LANES), row),
            pl.BlockSpec((TM_MERGE, LANES), row),
        ),
        compiler_params=_params("arbitrary"),
        name="merge_router",
    )(x2, sc1, sh1, gt1, gnm, *ys, wm, wb, wo, gnf, sc2, sh2, wr, br)


def _route(top_i, top_w, n_tok, n_tiles):
    tm = TM_EXPERT
    flat_e = top_i.reshape(-1)
    flat_w = top_w.reshape(-1)
    n_assign = flat_e.shape[0]
    order = jnp.argsort(flat_e, stable=True).astype(I32)
    counts = jnp.sum((flat_e[:, None] == jnp.arange(N_EXPERTS, dtype=I32)[None, :]).astype(I32), axis=0)
    cum_excl = jnp.cumsum(counts) - counts
    padded = ((counts + tm - 1) // tm) * tm
    gend = jnp.cumsum(padded)
    gstart = gend - padded
    tile_start = jnp.arange(n_tiles, dtype=I32) * tm
    tile_e_raw = jnp.sum((tile_start[:, None] >= gend[None, :]).astype(I32), axis=1)
    tile_e = jnp.minimum(tile_e_raw, N_EXPERTS - 1).astype(I32)
    tile_nv = jnp.where(tile_e_raw < N_EXPERTS, jnp.clip(counts[tile_e] - (tile_start - gstart[tile_e]), 0, tm), 0).astype(I32)
    slot = jnp.arange(n_tiles * tm, dtype=I32)
    e_s = jnp.repeat(tile_e, tm)
    j = slot - gstart[e_s]
    valid = (j < counts[e_s]) & (jnp.repeat(tile_e_raw, tm) < N_EXPERTS)
    a = order[jnp.clip(cum_excl[e_s] + j, 0, n_assign - 1)]
    tok = a // TOP_K
    kk = a % TOP_K
    src = jnp.where(valid, tok, 0).astype(I32)
    dst = jnp.where(valid, kk * n_tok + tok, TOP_K * n_tok + slot % tm).astype(I32)
    w = jnp.where(valid, flat_w[a], 0.0).astype(F32)
    return tile_e, tile_nv, src, dst, w.reshape(-1, 1)


def _moe_kernel(te_ref, nv_ref, src_hbm, dst_hbm, w_ref, h_hbm, wgu_ref, bgu_ref, wd_ref, bd_ref,
                y_hbm, src_s, dst_s, xs, ys, sem):
    i = pl.program_id(0)
    tm = TM_EXPERT
    d = xs.shape[1]

    nv = nv_ref[i]

    @pl.when(i == 0)
    def _():
        ys[...] = jnp.zeros(ys.shape, F32)
        fill = pltpu.make_async_copy(ys, y_hbm.at[pl.ds(y_hbm.shape[0] - tm, tm)], sem.at[3])
        fill.start()
        fill.wait()

    @pl.when(nv > 0)
    def _():
        base = pl.multiple_of(i * tm, tm)
        c_src = pltpu.make_async_copy(src_hbm.at[pl.ds(base, tm)], src_s, sem.at[0])
        c_dst = pltpu.make_async_copy(dst_hbm.at[pl.ds(base, tm)], dst_s, sem.at[1])
        c_src.start()
        c_dst.start()
        c_src.wait()
        c_dst.wait()

        def gather(r, carry):
            pltpu.make_async_copy(h_hbm.at[pl.ds(src_s[r], 1)], xs.at[pl.ds(r, 1)], sem.at[2]).start()
            return carry

        lax.fori_loop(0, tm, gather, 0, unroll=8)
        pltpu.make_async_copy(h_hbm.at[pl.ds(0, tm)], xs, sem.at[2]).wait()

        x = xs[...].astype(BF16)
        gu = _dot(x, wgu_ref[...]) + bgu_ref[...]
        x_glu = jnp.minimum(gu[:, :d], SWIGLU_LIMIT)
        x_lin = jnp.clip(gu[:, d:], -SWIGLU_LIMIT, SWIGLU_LIMIT)
        act = x_glu * jax.nn.sigmoid(SWIGLU_ALPHA * x_glu) * (x_lin + 1.0)
        out = _dot(act.astype(BF16), wd_ref[...]) + bd_ref[...]
        ys[...] = out * w_ref[...]

        def scatter(r, carry):
            pltpu.make_async_copy(ys.at[pl.ds(r, 1)], y_hbm.at[pl.ds(dst_s[r], 1)], sem.at[3]).start()
            return carry

        lax.fori_loop(0, tm, scatter, 0, unroll=8)
        pltpu.make_async_copy(ys, y_hbm.at[pl.ds(0, tm)], sem.at[3]).wait()


def _moe(h2, tile_e, tile_nv, src, dst, w_slot, wgu, bgu, wd, bd):
    t, d = h2.shape
    n_tiles = tile_e.shape[0]
    tm = TM_EXPERT
    any_spec = pl.BlockSpec(memory_space=pl.ANY)
    grid_spec = pltpu.PrefetchScalarGridSpec(
        num_scalar_prefetch=2,
        grid=(n_tiles,),
        in_specs=[
            any_spec, any_spec,
            pl.BlockSpec((tm, 1), lambda i, te, nu: (i, 0)),
            any_spec,
            pl.BlockSpec((None, d, 2 * d), lambda i, te, nu: (te[i], 0, 0)),
            pl.BlockSpec((None, 1, 2 * d), lambda i, te, nu: (te[i], 0, 0)),
            pl.BlockSpec((None, d, d), lambda i, te, nu: (te[i], 0, 0)),
            pl.BlockSpec((None, 1, d), lambda i, te, nu: (te[i], 0, 0)),
        ],
        out_specs=any_spec,
        scratch_shapes=[
            pltpu.SMEM((tm,), I32),
            pltpu.SMEM((tm,), I32),
            pltpu.VMEM((tm, d), F32),
            pltpu.VMEM((tm, d), F32),
            pltpu.SemaphoreType.DMA((4,)),
        ],
    )
    return pl.pallas_call(
        _moe_kernel,
        out_shape=jax.ShapeDtypeStruct((TOP_K * t + tm, d), F32),
        grid_spec=grid_spec,
        compiler_params=pltpu.CompilerParams(dimension_semantics=("arbitrary",), has_side_effects=True),
        name="moe_grouped",
    )(tile_e, tile_nv, src, dst, w_slot, h2, wgu, bgu, wd, bd)


def _combine_kernel(x_ref, gt_ref, y0_ref, y1_ref, y2_ref, y3_ref, gf_ref, o_ref, *, final):
    y = (y0_ref[...] + y1_ref[...]) + (y2_ref[...] + y3_ref[...])
    x = x_ref[...] + gt_ref[...] * y
    if final:
        ms = jnp.mean(x * x, axis=-1, keepdims=True)
        x = x * lax.rsqrt(ms + NORM_EPS) * gf_ref[...]
    o_ref[...] = x


def _combine(x1, gate2, planes, g_final, seq, final):
    t, d = x1.shape
    tiles_per_seq = seq // TM_COMBINE
    nt = t // TM_COMBINE
    row = lambda i: (i, 0)
    plane = lambda k: (lambda i: (k * nt + i, 0))
    blk = (TM_COMBINE, d)
    return pl.pallas_call(
        functools.partial(_combine_kernel, final=final),
        out_shape=jax.ShapeDtypeStruct((t, d), F32),
        grid=(nt,),
        in_specs=[
            pl.BlockSpec(blk, row),
            pl.BlockSpec((None, 1, d), lambda i: (i // tiles_per_seq, 0, 0)),
            pl.BlockSpec(blk, plane(0)), pl.BlockSpec(blk, plane(1)),
            pl.BlockSpec(blk, plane(2)), pl.BlockSpec(blk, plane(3)),
            pl.BlockSpec((1, d), lambda i: (0, 0)),
        ],
        out_specs=pl.BlockSpec(blk, row),
        compiler_params=_params("arbitrary"),
        name="moe_combine",
    )(x1, gate2, planes, planes, planes, planes, g_final)


def _pack_w_in(w):
    sizes = (256, 256, 256, 128, 128, 256, GLA_GATE_RANK, 256, 256, 256, 256, 256, 256, 128, 128)
    offs = np.concatenate([[0], np.cumsum(sizes)])
    mq, mk, mv, gq, gk, gv, ga, gr, rq, rk, rv, rg, sq, sk, sv = (w[:, int(offs[n]):int(offs[n + 1])] for n in range(15))
    merge = w[:, int(offs[15]):]
    d = w.shape[0]
    rep = lambda kv: jnp.repeat(kv.reshape(d, SWA_KV_HEADS, HEAD_DIM), BRANCH_HEADS // SWA_KV_HEADS, axis=1).reshape(d, BRANCH_WIDTH)
    scale = HEAD_DIM ** -0.5
    ga_pad = jnp.pad(ga, ((0, 0), (0, LANES - GLA_GATE_RANK)))
    packed = jnp.concatenate([mq * scale, mk, mv, gq, gk, gv, ga_pad, gr, rq, rk * scale, rv, rg,
                              sq * scale, rep(sk), rep(sv)], axis=1)
    return packed.astype(BF16), merge.astype(BF16)


def kernel(x, c, w_ada, b_ada, g_norm_mix, w_in, w_gla_gate, b_gla_gate, g_gla_norm, g_ret_norm, attn_sinks,
           w_branch, w_out, g_norm_ffn, w_router, b_router, w_gate_up, b_gate_up, w_down, b_down, g_final):
    batch, seq, d = x.shape
    depth = w_ada.shape[0]
    t = batch * seq
    n_alibi = 2 * BRANCH_HEADS
    slopes = [2.0 ** (-(k + 1.0) * (8.0 / n_alibi)) for k in range(n_alibi)]
    swa_slopes, moba_slopes = tuple(slopes[:BRANCH_HEADS]), tuple(slopes[BRANCH_HEADS:])
    log_gamma = jnp.log(1.0 - 2.0 ** (-RET_DECAY_BASE - jnp.arange(BRANCH_HEADS, dtype=F32)))
    log_gamma_row = jnp.repeat(log_gamma, HEAD_DIM).reshape(1, BRANCH_WIDTH)
    dummy_row = jnp.zeros((1, BRANCH_HEADS * GLA_KEY_DIM), F32)
    n_tiles = (TOP_K * t + N_EXPERTS * (TM_EXPERT - 1) + TM_EXPERT - 1) // TM_EXPERT

    mod = _modulation(c, w_ada, b_ada)
    xf = x.reshape(t, d)
    for l in range(depth):
        shift1, scale1, gate1, shift2, scale2, gate2 = (
            mod[l, :, n * d:(n + 1) * d].reshape(batch, 1, d) for n in range(6))
        w_packed, w_merge = _pack_w_in(w_in[l])
        wg = jnp.pad(w_gla_gate[l], ((0, LANES - GLA_GATE_RANK), (0, 0)))
        moba_qkv, gla_f, gla_v, ret_f, ret_v, swa_qkv, kmean = _input_projection(
            xf, scale1, shift1, g_norm_mix[l].reshape(1, d), w_packed, wg, b_gla_gate[l].reshape(1, -1), seq)
        kmean = kmean.reshape(batch, seq // MOBA_BLOCK, BRANCH_WIDTH)

        y_moba = _moba(moba_qkv, kmean, batch, seq, moba_slopes)
        y_gla = _linear_attention(gla_f, gla_v, g_gla_norm[l].reshape(1, -1), dummy_row, batch, seq,
                                  dk=GLA_KEY_DIM, gated=True, qoff=0, koff=128, laoff=256, roff=384,
                                  norm="rms", name="gla")
        y_ret = _linear_attention(ret_f, ret_v, g_ret_norm[l].reshape(1, -1), log_gamma_row, batch, seq,
                                  dk=HEAD_DIM, gated=False, qoff=0, koff=256, laoff=0, roff=512,
                                  norm="group", name="retention")
        sinks_row = jnp.pad(attn_sinks[l].reshape(1, -1), ((0, 0), (0, LANES - BRANCH_HEADS)))
        y_swa = _swa(swa_qkv, sinks_row, batch, seq, swa_slopes)

        wr = jnp.pad(w_router[l], ((0, 0), (0, LANES - N_EXPERTS)))
        br = jnp.pad(b_router[l].reshape(1, -1), ((0, 0), (0, LANES - N_EXPERTS)), constant_values=NEG_BIG)
        x1, h2, top_i, top_w = _merge(
            xf, scale1, shift1, gate1, g_norm_mix[l].reshape(1, d), (y_moba, y_gla, y_ret, y_swa),
            w_merge, w_branch[l].astype(BF16), w_out[l].astype(BF16),
            g_norm_ffn[l].reshape(1, d), scale2, shift2, wr, br, seq)

        tile_e, tile_nv, src, dst, w_slot = _route(top_i[:, :TOP_K], top_w[:, :TOP_K], t, n_tiles)
        planes = _moe(h2, tile_e, tile_nv, src, dst, w_slot,
                      w_gate_up[l].astype(BF16), b_gate_up[l].reshape(N_EXPERTS, 1, -1),
                      w_down[l].astype(BF16), b_down[l].reshape(N_EXPERTS, 1, -1))
        xf = _combine(x1, gate2, planes, g_final.reshape(1, d), seq, final=(l == depth - 1))
    return xf.reshape(batch, seq, d)
```

```python
import functools

import jax
import jax.numpy as jnp
import numpy as np
from jax import lax
from jax.experimental import pallas as pl
from jax.experimental.pallas import tpu as pltpu

F32 = jnp.float32
BF16 = jnp.bfloat16
I32 = jnp.int32
HI = lax.Precision.HIGHEST

HEAD_DIM = 64
N_BRANCHES = 4
BRANCH_HEADS = 4
BRANCH_WIDTH = HEAD_DIM * BRANCH_HEADS
MOBA_BLOCK = 256
MOBA_TOPK = 3
GLA_KEY_DIM = 32
GLA_GATE_RANK = 16
GLA_GATE_TEMP = 16.0
LINEAR_CHUNK = 64
RET_DECAY_BASE = 5.0
SWA_KV_HEADS = 2
WINDOW = 128
N_EXPERTS = 32
TOP_K = 4
SWIGLU_ALPHA = 1.702
SWIGLU_LIMIT = 7.0
NORM_EPS = 1e-5

LANES = 128
TM_IN = 512
TM_MERGE = 256
TM_EXPERT = 256
TM_COMBINE = 512
LIN_GROUP = 256
MOD_COLS = 1536
NEG_BIG = -1e30

_C_MOBA = (0, 768)
_C_GLA = (768, 1664)
_C_RET = (1664, 2688)
_C_SWA = (2688, 3456)
N_PACKED = 3456


def _dot(a, b, precision=None):
    return jnp.dot(a, b, preferred_element_type=F32, precision=precision)


def _dot_nt(a, b, precision=None):
    return lax.dot_general(a, b, (((1,), (1,)), ((), ())), preferred_element_type=F32, precision=precision)


def _dot_tn(a, b, precision=None):
    return lax.dot_general(a, b, (((0,), (0,)), ((), ())), preferred_element_type=F32, precision=precision)


def _norm_mod(x, g, scale, shift):
    ms = jnp.mean(x * x, axis=-1, keepdims=True)
    y = x * lax.rsqrt(ms + NORM_EPS) * g
    return y * (1.0 + scale) + shift


def _log_sigmoid(x):
    return jnp.minimum(x, 0.0) - jnp.log1p(jnp.exp(-jnp.abs(x)))


def _params(*sem):
    return pltpu.CompilerParams(dimension_semantics=tuple(sem))


def _mod_kernel(c_ref, w_ref, b_ref, o_ref):
    c = c_ref[...]
    o_ref[...] = _dot(jax.nn.silu(c), w_ref[...], HI) + b_ref[...]


def _modulation(c, w_ada, b_ada):
    depth, d, n = w_ada.shape
    b = c.shape[0]
    return pl.pallas_call(
        _mod_kernel,
        out_shape=jax.ShapeDtypeStruct((depth, b, n), F32),
        grid=(depth, n // MOD_COLS),
        in_specs=[
            pl.BlockSpec((b, d), lambda l, j: (0, 0)),
            pl.BlockSpec((None, d, MOD_COLS), lambda l, j: (l, 0, j)),
            pl.BlockSpec((None, 1, MOD_COLS), lambda l, j: (l, 0, j)),
        ],
        out_specs=pl.BlockSpec((None, b, MOD_COLS), lambda l, j: (l, 0, j)),
        compiler_params=_params("arbitrary", "arbitrary"),
        name="adaln_mod",
    )(c, w_ada, b_ada.reshape(depth, 1, n))


def _in_kernel(x_ref, sc_ref, sh_ref, g_ref, w_ref, wg_ref, bg_ref,
               moba_ref, glaf_ref, glav_ref, retf_ref, retv_ref, swa_ref, kmean_ref):
    h = _norm_mod(x_ref[...], g_ref[...], sc_ref[...], sh_ref[...]).astype(BF16)

    zm = _dot(h, w_ref[:, _C_MOBA[0]:_C_MOBA[1]])
    moba_ref[...] = zm.astype(BF16)
    for blk in range(TM_IN // MOBA_BLOCK):
        kb = zm[blk * MOBA_BLOCK:(blk + 1) * MOBA_BLOCK, 256:512]
        kmean_ref[blk:blk + 1, :] = jnp.mean(kb, axis=0, keepdims=True)

    zg = _dot(h, w_ref[:, _C_GLA[0]:_C_GLA[1]])
    gate_logit = _dot(zg[:, 512:640], wg_ref[...], HI) + bg_ref[...]
    glaf_ref[:, 0:128] = zg[:, 0:128] * (GLA_KEY_DIM ** -0.5)
    glaf_ref[:, 128:256] = zg[:, 128:256]
    glaf_ref[:, 256:384] = _log_sigmoid(gate_logit) / GLA_GATE_TEMP
    glaf_ref[:, 384:640] = zg[:, 640:896]
    glav_ref[...] = zg[:, 256:512].astype(BF16)

    zr = _dot(h, w_ref[:, _C_RET[0]:_C_RET[1]])
    retf_ref[:, 0:512] = zr[:, 0:512]
    retf_ref[:, 512:768] = zr[:, 768:1024]
    retv_ref[...] = zr[:, 512:768].astype(BF16)

    swa_ref[...] = _dot(h, w_ref[:, _C_SWA[0]:_C_SWA[1]]).astype(BF16)


def _input_projection(x2, scale, shift, g, w_packed, wg, bg, seq):
    t, d = x2.shape
    tiles_per_seq = seq // TM_IN
    nblk = TM_IN // MOBA_BLOCK
    row = lambda i: (i, 0)
    per_batch = lambda i: (i // tiles_per_seq, 0, 0)
    const2 = lambda i: (0, 0)
    outs = pl.pallas_call(
        _in_kernel,
        out_shape=(
            jax.ShapeDtypeStruct((t, 768), BF16),
            jax.ShapeDtypeStruct((t, 640), F32),
            jax.ShapeDtypeStruct((t, 256), BF16),
            jax.ShapeDtypeStruct((t, 768), F32),
            jax.ShapeDtypeStruct((t, 256), BF16),
            jax.ShapeDtypeStruct((t, 768), BF16),
            jax.ShapeDtypeStruct((t // TM_IN, nblk, 256), F32),
        ),
        grid=(t // TM_IN,),
        in_specs=[
            pl.BlockSpec((TM_IN, d), row),
            pl.BlockSpec((None, 1, d), per_batch),
            pl.BlockSpec((None, 1, d), per_batch),
            pl.BlockSpec((1, d), const2),
            pl.BlockSpec((d, N_PACKED), const2),
            pl.BlockSpec((LANES, LANES), const2),
            pl.BlockSpec((1, LANES), const2),
        ],
        out_specs=(
            pl.BlockSpec((TM_IN, 768), row),
            pl.BlockSpec((TM_IN, 640), row),
            pl.BlockSpec((TM_IN, 256), row),
            pl.BlockSpec((TM_IN, 768), row),
            pl.BlockSpec((TM_IN, 256), row),
            pl.BlockSpec((TM_IN, 768), row),
            pl.BlockSpec((None, nblk, 256), lambda i: (i, 0, 0)),
        ),
        compiler_params=_params("arbitrary"),
        name="input_proj",
    )(x2, scale, shift, g, w_packed, wg, bg)
    return outs


def _moba_kernel(q_ref, k_ref, v_ref, km_ref, o_ref, m_sc, l_sc, acc_sc, *, slopes):
    i = pl.program_id(1)
    nb = MOBA_BLOCK
    n_blocks = km_ref.shape[0]
    q = q_ref[...]
    lane = lax.broadcasted_iota(I32, (1, BRANCH_WIDTH), 1)
    head_of_lane = lane // HEAD_DIM

    rows = lax.broadcasted_iota(I32, (LANES, BRANCH_WIDTH), 0)
    lanes_full = lax.broadcasted_iota(I32, (LANES, BRANCH_WIDTH), 1)
    km = km_ref[...]
    km_t = jnp.concatenate([km] * (LANES // n_blocks), axis=0)
    km_heads = jnp.where((rows // n_blocks) == (lanes_full // HEAD_DIM), km_t, 0.0)
    gate_t = _dot_nt(km_heads, q.astype(F32), HI)

    jrow = lax.broadcasted_iota(I32, (n_blocks, nb), 0)
    valid = jrow < i
    sel_rows = []
    for h in range(BRANCH_HEADS):
        g = jnp.where(valid, gate_t[h * n_blocks:(h + 1) * n_blocks, :], -jnp.inf)
        cnt = jnp.zeros((n_blocks, nb), F32)
        for jp in range(n_blocks):
            gj = g[jp:jp + 1, :]
            beats = (gj > g) | ((gj == g) & (jp < jrow))
            cnt = cnt + beats.astype(F32)
        sel_rows.append(((cnt < MOBA_TOPK) & valid).astype(F32))
    sel_rows.append(jnp.zeros((LANES - BRANCH_HEADS * n_blocks, nb), F32))
    sel_t = jnp.concatenate(sel_rows, axis=0).T

    r_io = lax.broadcasted_iota(I32, (nb, nb), 0)
    c_io = lax.broadcasted_iota(I32, (nb, nb), 1)
    rel = (r_io - c_io).astype(F32)
    lane128 = lax.broadcasted_iota(I32, (nb, LANES), 1)

    m_sc[...] = jnp.full(m_sc.shape, -jnp.inf, F32)
    l_sc[...] = jnp.zeros(l_sc.shape, F32)
    acc_sc[...] = jnp.zeros(acc_sc.shape, F32)

    def process(j, own):
        start = pl.multiple_of(j * nb, nb)
        kj = k_ref[pl.ds(start, nb), :]
        vj = v_ref[pl.ds(start, nb), :]
        dist = rel + ((i - j) * nb).astype(F32)
        for h in range(BRANCH_HEADS):
            hm = head_of_lane == h
            s = _dot_nt(jnp.where(hm, q, jnp.zeros_like(q)), kj) - slopes[h] * dist
            if own:
                mask = rel >= 0.0
            else:
                col = jnp.sum(jnp.where(lane128 == h * n_blocks + j, sel_t, 0.0), axis=1, keepdims=True)
                mask = col > 0.5
            s = jnp.where(mask, s, -jnp.inf)
            m_old = m_sc[h]
            m_new = jnp.maximum(m_old, jnp.max(s, axis=1, keepdims=True))
            alpha = jnp.exp(m_old - m_new)
            p = jnp.exp(s - m_new)
            l_sc[h] = alpha * l_sc[h] + jnp.sum(p, axis=1, keepdims=True)
            m_sc[h] = m_new
            pv = _dot(p.astype(BF16), jnp.where(hm, vj, jnp.zeros_like(vj)))
            acc_sc[...] = acc_sc[...] * jnp.where(hm, alpha, 1.0) + pv

    process(i, True)

    def body(j, carry):
        process(j, False)
        return carry

    lax.fori_loop(0, i, body, 0)

    inv = jnp.zeros((nb, BRANCH_WIDTH), F32)
    for h in range(BRANCH_HEADS):
        inv = jnp.where(head_of_lane == h, 1.0 / l_sc[h], inv)
    o_ref[...] = (acc_sc[...] * inv).astype(BF16)


def _moba(qkv, kmean, batch, seq, slopes):
    t = qkv.shape[0]
    nq = seq // MOBA_BLOCK
    kern = functools.partial(_moba_kernel, slopes=slopes)
    return pl.pallas_call(
        kern,
        out_shape=jax.ShapeDtypeStruct((t, BRANCH_WIDTH), BF16),
        grid=(batch, nq),
        in_specs=[
            pl.BlockSpec((MOBA_BLOCK, BRANCH_WIDTH), lambda b, i: (b * nq + i, 0)),
            pl.BlockSpec((seq, BRANCH_WIDTH), lambda b, i: (b, 1)),
            pl.BlockSpec((seq, BRANCH_WIDTH), lambda b, i: (b, 2)),
            pl.BlockSpec((None, nq, BRANCH_WIDTH), lambda b, i: (b, 0, 0)),
        ],
        out_specs=pl.BlockSpec((MOBA_BLOCK, BRANCH_WIDTH), lambda b, i: (b * nq + i, 0)),
        scratch_shapes=[
            pltpu.VMEM((BRANCH_HEADS, MOBA_BLOCK, 1), F32),
            pltpu.VMEM((BRANCH_HEADS, MOBA_BLOCK, 1), F32),
            pltpu.VMEM((MOBA_BLOCK, BRANCH_WIDTH), F32),
        ],
        compiler_params=_params("arbitrary", "arbitrary"),
        name="moba_attn",
    )(qkv, qkv, qkv, kmean)


def _swa_kernel(q_ref, kp_ref, kc_ref, vp_ref, vc_ref, sink_ref, o_ref, *, slopes):
    n = pl.program_id(1)
    w = WINDOW
    q = q_ref[...]
    kp, kc, vp, vc = kp_ref[...], kc_ref[...], vp_ref[...], vc_ref[...]
    head_of_lane = lax.broadcasted_iota(I32, (1, BRANCH_WIDTH), 1) // HEAD_DIM
    r_io = lax.broadcasted_iota(I32, (w, w), 0)
    c_io = lax.broadcasted_iota(I32, (w, w), 1)
    rel = (r_io - c_io).astype(F32)
    allow_cur = rel >= 0.0
    allow_prev = (rel < 0.0) & (n > 0)
    out = jnp.zeros((w, BRANCH_WIDTH), F32)
    zq = jnp.zeros_like(q)
    zv = jnp.zeros_like(vp)
    for h in range(BRANCH_HEADS):
        hm = head_of_lane == h
        qh = jnp.where(hm, q, zq)
        sp = jnp.where(allow_prev, _dot_nt(qh, kp) - slopes[h] * (rel + float(w)), -jnp.inf)
        sc = jnp.where(allow_cur, _dot_nt(qh, kc) - slopes[h] * rel, -jnp.inf)
        sink = sink_ref[0:1, h:h + 1]
        m = jnp.maximum(jnp.maximum(jnp.max(sp, axis=1, keepdims=True), jnp.max(sc, axis=1, keepdims=True)), sink)
        pp = jnp.exp(sp - m)
        pc = jnp.exp(sc - m)
        l = jnp.sum(pp, axis=1, keepdims=True) + jnp.sum(pc, axis=1, keepdims=True) + jnp.exp(sink - m)
        o = _dot(pp.astype(BF16), jnp.where(hm, vp, zv)) + _dot(pc.astype(BF16), jnp.where(hm, vc, zv))
        out = out + o * (1.0 / l)
    o_ref[...] = out.astype(BF16)


def _swa(qkv, sinks_row, batch, seq, slopes):
    t = qkv.shape[0]
    nq = seq // WINDOW
    kern = functools.partial(_swa_kernel, slopes=slopes)
    cur = lambda col: (lambda b, n: (b * nq + n, col))
    prev = lambda col: (lambda b, n: (b * nq + jnp.maximum(n - 1, 0), col))
    blk = (WINDOW, BRANCH_WIDTH)
    return pl.pallas_call(
        kern,
        out_shape=jax.ShapeDtypeStruct((t, BRANCH_WIDTH), BF16),
        grid=(batch, nq),
        in_specs=[
            pl.BlockSpec(blk, cur(0)),
            pl.BlockSpec(blk, prev(1)),
            pl.BlockSpec(blk, cur(1)),
            pl.BlockSpec(blk, prev(2)),
            pl.BlockSpec(blk, cur(2)),
            pl.BlockSpec((1, LANES), lambda b, n: (0, 0)),
        ],
        out_specs=pl.BlockSpec(blk, cur(0)),
        compiler_params=_params("arbitrary", "arbitrary"),
        name="swa_attn",
    )(qkv, qkv, qkv, qkv, qkv, sinks_row)


def _lin_kernel(f_ref, v_ref, g_ref, lg_ref, o_ref, st_sc, *, dk, gated, qoff, koff, laoff, roff, norm):
    width = BRANCH_HEADS * dk
    grp = LIN_GROUP
    ch = LINEAR_CHUNK
    seq = f_ref.shape[0]

    r_io = lax.broadcasted_iota(I32, (grp, grp), 0)
    c_io = lax.broadcasted_iota(I32, (grp, grp), 1)
    same = (r_io // ch) == (c_io // ch)
    ltri = (same & (c_io <= r_io)).astype(F32)
    lfull = same.astype(F32)
    head_avg = ((r_io // HEAD_DIM) == (c_io // HEAD_DIM)).astype(F32) * (1.0 / HEAD_DIM)
    lane_k = lax.broadcasted_iota(I32, (1, width), 1) // dk
    lane_v = lax.broadcasted_iota(I32, (1, BRANCH_WIDTH), 1) // HEAD_DIM
    bd_mask = (lax.broadcasted_iota(I32, (BRANCH_WIDTH, width), 0) // HEAD_DIM
               == lax.broadcasted_iota(I32, (BRANCH_WIDTH, width), 1) // dk)
    gain = g_ref[...]

    st_sc[...] = jnp.zeros(st_sc.shape, F32)

    def group(gi, carry):
        r0 = pl.multiple_of(gi * grp, grp)
        rows = pl.ds(r0, grp)
        q = f_ref[rows, qoff:qoff + width]
        k = f_ref[rows, koff:koff + width]
        if gated:
            la = f_ref[rows, laoff:laoff + width]
        else:
            la = jnp.broadcast_to(lg_ref[...], (grp, width))
        b = _dot(ltri, la, HI)
        be = _dot(lfull, la, HI)
        qd = (q * jnp.exp(b)).astype(BF16)
        kin = (k * jnp.exp(-b)).astype(BF16)
        kend = (k * jnp.exp(be - b)).astype(BF16)
        dec = jnp.exp(be)
        v = v_ref[rows, :]
        zq = jnp.zeros_like(qd)
        zv = jnp.zeros_like(v)

        o = jnp.zeros((grp, BRANCH_WIDTH), F32)
        for h in range(BRANCH_HEADS):
            a = _dot_nt(jnp.where(lane_k == h, qd, zq), kin) * ltri
            o = o + _dot(a.astype(BF16), jnp.where(lane_v == h, v, zv))

        st = st_sc[...]
        parts = []
        for c in range(grp // ch):
            sl = slice(c * ch, (c + 1) * ch)
            parts.append(_dot_nt(qd[sl], st.astype(BF16)))
            kv_t = _dot_tn(v[sl], kend[sl])
            st = st * dec[c * ch:c * ch + 1, :] + jnp.where(bd_mask, kv_t, 0.0)
        st_sc[...] = st
        o = o + jnp.concatenate(parts, axis=0)

        if norm == "rms":
            ms = _dot(o * o, head_avg, HI)
            y = o * lax.rsqrt(ms + NORM_EPS) * gain
        else:
            mu = _dot(o, head_avg, HI)
            xc = o - mu
            var = _dot(xc * xc, head_avg, HI)
            y = xc * lax.rsqrt(var + NORM_EPS) * gain
        r = f_ref[rows, roff:roff + BRANCH_WIDTH]
        o_ref[rows, :] = (y * jax.nn.silu(r)).astype(BF16)
        return carry

    lax.fori_loop(0, seq // grp, group, 0)


def _linear_attention(feat, v, gain, log_gamma_row, batch, seq, *, dk, gated, qoff, koff, laoff, roff, norm, name):
    t, fw = feat.shape
    width = BRANCH_HEADS * dk
    kern = functools.partial(_lin_kernel, dk=dk, gated=gated, qoff=qoff, koff=koff, laoff=laoff, roff=roff, norm=norm)
    return pl.pallas_call(
        kern,
        out_shape=jax.ShapeDtypeStruct((t, BRANCH_WIDTH), BF16),
        grid=(batch,),
        in_specs=[
            pl.BlockSpec((seq, fw), lambda b: (b, 0)),
            pl.BlockSpec((seq, BRANCH_WIDTH), lambda b: (b, 0)),
            pl.BlockSpec((1, BRANCH_WIDTH), lambda b: (0, 0)),
            pl.BlockSpec((1, width), lambda b: (0, 0)),
        ],
        out_specs=pl.BlockSpec((seq, BRANCH_WIDTH), lambda b: (b, 0)),
        scratch_shapes=[pltpu.VMEM((BRANCH_WIDTH, width), F32)],
        compiler_params=_params("arbitrary"),
        name=name,
    )(feat, v, gain, log_gamma_row)


def _merge_kernel(x_ref, sc1_ref, sh1_ref, gt1_ref, gnm_ref, ym_ref, yg_ref, yr_ref, ys_ref,
                  wm_ref, wb_ref, wo_ref, gnf_ref, sc2_ref, sh2_ref, wr_ref, br_ref,
                  x1_ref, h2_ref, ti_ref, tw_ref):
    x = x_ref[...]
    d = x.shape[1]
    hb = _norm_mod(x, gnm_ref[...], sc1_ref[...], sh1_ref[...]).astype(BF16)
    mixed = jnp.zeros(x.shape, F32)
    for n, y_ref in enumerate((ym_ref, yg_ref, yr_ref, ys_ref)):
        gate = jax.nn.sigmoid(_dot(hb, wm_ref[:, n * d:(n + 1) * d]))
        mixed = mixed + gate * _dot(y_ref[...], wb_ref[n])
    x1 = x + gt1_ref[...] * _dot(mixed.astype(BF16), wo_ref[...])
    x1_ref[...] = x1

    h2 = _norm_mod(x1, gnf_ref[...], sc2_ref[...], sh2_ref[...])
    h2_ref[...] = h2
    logits = _dot(h2, wr_ref[...], HI) + br_ref[...]
    lane = lax.broadcasted_iota(I32, logits.shape, 1)
    vals, idxs = [], []
    cur = logits
    for _ in range(TOP_K):
        m = jnp.max(cur, axis=1, keepdims=True)
        idx = jnp.min(jnp.where(cur == m, lane, LANES), axis=1, keepdims=True)
        vals.append(m)
        idxs.append(idx)
        cur = jnp.where(lane == idx, -jnp.inf, cur)
    es = [jnp.exp(v - vals[0]) for v in vals]
    tot = es[0] + es[1] + es[2] + es[3]
    ti = jnp.zeros(logits.shape, I32)
    tw = jnp.zeros(logits.shape, F32)
    for k in range(TOP_K):
        ti = jnp.where(lane == k, idxs[k], ti)
        tw = jnp.where(lane == k, es[k] / tot, tw)
    ti_ref[...] = ti
    tw_ref[...] = tw


def _merge(x2, sc1, sh1, gt1, gnm, ys, wm, wb, wo, gnf, sc2, sh2, wr, br, seq):
    t, d = x2.shape
    tiles_per_seq = seq // TM_MERGE
    row = lambda i: (i, 0)
    per_batch = lambda i: (i // tiles_per_seq, 0, 0)
    c2 = lambda i: (0, 0)
    c3 = lambda i: (0, 0, 0)
    vec = pl.BlockSpec((None, 1, d), per_batch)
    ytile = pl.BlockSpec((TM_MERGE, BRANCH_WIDTH), row)
    return pl.pallas_call(
        _merge_kernel,
        out_shape=(
            jax.ShapeDtypeStruct((t, d), F32),
            jax.ShapeDtypeStruct((t, d), F32),
            jax.ShapeDtypeStruct((t, LANES), I32),
            jax.ShapeDtypeStruct((t, LANES), F32),
        ),
        grid=(t // TM_MERGE,),
        in_specs=[
            pl.BlockSpec((TM_MERGE, d), row), vec, vec, vec,
            pl.BlockSpec((1, d), c2),
            ytile, ytile, ytile, ytile,
            pl.BlockSpec((d, N_BRANCHES * d), c2),
            pl.BlockSpec((N_BRANCHES, BRANCH_WIDTH, d), c3),
            pl.BlockSpec((d, d), c2),
            pl.BlockSpec((1, d), c2), vec, vec,
            pl.BlockSpec((d, LANES), c2),
            pl.BlockSpec((1, LANES), c2),
        ],
        out_specs=(
            pl.BlockSpec((TM_MERGE, d), row),
            pl.BlockSpec((TM_MERGE, d), row),
            pl.BlockSpec((TM_MERGE, LANES), row),
            pl.BlockSpec((TM_MERGE, LANES), row),
        ),
        compiler_params=_params("arbitrary"),
        name="merge_router",
    )(x2, sc1, sh1, gt1, gnm, *ys, wm, wb, wo, gnf, sc2, sh2, wr, br)


def _route(top_i, top_w, n_tok, n_tiles):
    tm = TM_EXPERT
    flat_e = top_i.reshape(-1)
    flat_w = top_w.reshape(-1)
    n_assign = flat_e.shape[0]
    sorted_e, order = lax.sort((flat_e, jnp.arange(n_assign, dtype=I32)), num_keys=1, is_stable=True)
    bounds = jnp.searchsorted(sorted_e, jnp.arange(N_EXPERTS + 1, dtype=I32)).astype(I32)
    counts = bounds[1:] - bounds[:-1]
    cum_excl = bounds[:-1]
    padded = ((counts + tm - 1) // tm) * tm
    gend = jnp.cumsum(padded)
    gstart = gend - padded
    tile_start = jnp.arange(n_tiles, dtype=I32) * tm
    tile_e_raw = jnp.sum((tile_start[:, None] >= gend[None, :]).astype(I32), axis=1)
    tile_e = jnp.minimum(tile_e_raw, N_EXPERTS - 1).astype(I32)
    n_used = (gend[-1] // tm).astype(I32).reshape(1)
    slot = jnp.arange(n_tiles * tm, dtype=I32)
    e_s = jnp.repeat(tile_e, tm)
    j = slot - gstart[e_s]
    valid = (j < counts[e_s]) & (jnp.repeat(tile_e_raw, tm) < N_EXPERTS)
    a = order[jnp.clip(cum_excl[e_s] + j, 0, n_assign - 1)]
    tok = a // TOP_K
    kk = a % TOP_K
    src = jnp.where(valid, tok, 0).astype(I32)
    dst = jnp.where(valid, kk * n_tok + tok, TOP_K * n_tok + slot % tm).astype(I32)
    w = jnp.where(valid, flat_w[a], 0.0).astype(F32)
    return tile_e, n_used, src, dst, w.reshape(-1, 1)


def _moe_kernel(te_ref, nu_ref, src_hbm, dst_hbm, w_ref, h_hbm, wgu_ref, bgu_ref, wd_ref, bd_ref,
                y_hbm, src_s, dst_s, xs, ys, wgu_b, wd_b, isem, gsem, ssem):
    i = pl.program_id(0)
    tm = TM_EXPERT
    d = xs.shape[2]
    n_used = nu_ref[0]

    def idx_copies(tile, buf):
        base = pl.multiple_of(tile * tm, tm)
        return (pltpu.make_async_copy(src_hbm.at[pl.ds(base, tm)], src_s.at[buf], isem.at[buf, 0]),
                pltpu.make_async_copy(dst_hbm.at[pl.ds(base, tm)], dst_s.at[buf], isem.at[buf, 1]))

    def start_idx(tile, buf):
        for cp in idx_copies(tile, buf):
            cp.start()

    def wait_idx(tile, buf):
        for cp in idx_copies(tile, buf):
            cp.wait()

    def issue_gather(buf, slot):
        def body(r, carry):
            pltpu.make_async_copy(h_hbm.at[pl.ds(src_s[buf, r], 1)], xs.at[slot, pl.ds(r, 1)], gsem.at[slot]).start()
            return carry
        lax.fori_loop(0, tm, body, 0, unroll=8)

    def wait_gather(slot):
        pltpu.make_async_copy(h_hbm.at[pl.ds(0, tm)], xs.at[slot], gsem.at[slot]).wait()

    def wait_scatter():
        pltpu.make_async_copy(ys, y_hbm.at[pl.ds(0, tm)], ssem.at[0]).wait()

    @pl.when(i == 0)
    def _():
        ys[...] = jnp.zeros(ys.shape, F32)
        fill = pltpu.make_async_copy(ys, y_hbm.at[pl.ds(y_hbm.shape[0] - tm, tm)], ssem.at[0])
        fill.start()
        fill.wait()

        @pl.when(n_used > 0)
        def _():
            start_idx(0, 0)
            wait_idx(0, 0)
            issue_gather(0, 0)

        @pl.when(n_used > 1)
        def _():
            start_idx(1, 1)

    @pl.when(i < n_used)
    def _():
        slot = lax.rem(i, 2)
        buf = lax.rem(i, 3)

        @pl.when(i + 1 < n_used)
        def _():
            nbuf = lax.rem(i + 1, 3)
            wait_idx(i + 1, nbuf)
            issue_gather(nbuf, 1 - slot)

        @pl.when(i + 2 < n_used)
        def _():
            start_idx(i + 2, lax.rem(i + 2, 3))

        @pl.when((i == 0) | (te_ref[i] != te_ref[jnp.maximum(i - 1, 0)]))
        def _():
            rows = d // 8

            def cast(c, carry):
                r = pl.ds(pl.multiple_of(c * rows, rows), rows)
                wgu_b[r, :] = wgu_ref[r, :].astype(BF16)
                wd_b[r, :] = wd_ref[r, :].astype(BF16)
                return carry
            lax.fori_loop(0, 8, cast, 0)

        wait_gather(slot)
        x = xs[slot].astype(BF16)
        gu = _dot(x, wgu_b[...]) + bgu_ref[...]
        x_glu = jnp.minimum(gu[:, :d], SWIGLU_LIMIT)
        x_lin = jnp.clip(gu[:, d:], -SWIGLU_LIMIT, SWIGLU_LIMIT)
        act = x_glu * jax.nn.sigmoid(SWIGLU_ALPHA * x_glu) * (x_lin + 1.0)
        out = (_dot(act.astype(BF16), wd_b[...]) + bd_ref[...]) * w_ref[...]

        @pl.when(i > 0)
        def _():
            wait_scatter()

        ys[...] = out

        def scatter(r, carry):
            pltpu.make_async_copy(ys.at[pl.ds(r, 1)], y_hbm.at[pl.ds(dst_s[buf, r], 1)], ssem.at[0]).start()
            return carry
        lax.fori_loop(0, tm, scatter, 0, unroll=8)

        @pl.when(i == n_used - 1)
        def _():
            wait_scatter()


def _moe(h2, tile_e, n_used, src, dst, w_slot, wgu, bgu, wd, bd):
    t, d = h2.shape
    n_tiles = tile_e.shape[0]
    tm = TM_EXPERT
    any_spec = pl.BlockSpec(memory_space=pl.ANY)
    grid_spec = pltpu.PrefetchScalarGridSpec(
        num_scalar_prefetch=2,
        grid=(n_tiles,),
        in_specs=[
            any_spec, any_spec,
            pl.BlockSpec((tm, 1), lambda i, te, nu: (i, 0)),
            any_spec,
            pl.BlockSpec((None, d, 2 * d), lambda i, te, nu: (te[i], 0, 0)),
            pl.BlockSpec((None, 1, 2 * d), lambda i, te, nu: (te[i], 0, 0)),
            pl.BlockSpec((None, d, d), lambda i, te, nu: (te[i], 0, 0)),
            pl.BlockSpec((None, 1, d), lambda i, te, nu: (te[i], 0, 0)),
        ],
        out_specs=any_spec,
        scratch_shapes=[
            pltpu.SMEM((3, tm), I32),
            pltpu.SMEM((3, tm), I32),
            pltpu.VMEM((2, tm, d), F32),
            pltpu.VMEM((tm, d), F32),
            pltpu.VMEM((d, 2 * d), BF16),
            pltpu.VMEM((d, d), BF16),
            pltpu.SemaphoreType.DMA((3, 2)),
            pltpu.SemaphoreType.DMA((2,)),
            pltpu.SemaphoreType.DMA((1,)),
        ],
    )
    return pl.pallas_call(
        _moe_kernel,
        out_shape=jax.ShapeDtypeStruct((TOP_K * t + tm, d), F32),
        grid_spec=grid_spec,
        compiler_params=pltpu.CompilerParams(dimension_semantics=("arbitrary",), has_side_effects=True),
        name="moe_grouped",
    )(tile_e, n_used, src, dst, w_slot, h2, wgu, bgu, wd, bd)


def _combine_kernel(x_ref, gt_ref, y0_ref, y1_ref, y2_ref, y3_ref, gf_ref, o_ref, *, final):
    y = (y0_ref[...] + y1_ref[...]) + (y2_ref[...] + y3_ref[...])
    x = x_ref[...] + gt_ref[...] * y
    if final:
        ms = jnp.mean(x * x, axis=-1, keepdims=True)
        x = x * lax.rsqrt(ms + NORM_EPS) * gf_ref[...]
    o_ref[...] = x


def _combine(x1, gate2, planes, g_final, seq, final):
    t, d = x1.shape
    tiles_per_seq = seq // TM_COMBINE
    nt = t // TM_COMBINE
    row = lambda i: (i, 0)
    plane = lambda k: (lambda i: (k * nt + i, 0))
    blk = (TM_COMBINE, d)
    return pl.pallas_call(
        functools.partial(_combine_kernel, final=final),
        out_shape=jax.ShapeDtypeStruct((t, d), F32),
        grid=(nt,),
        in_specs=[
            pl.BlockSpec(blk, row),
            pl.BlockSpec((None, 1, d), lambda i: (i // tiles_per_seq, 0, 0)),
            pl.BlockSpec(blk, plane(0)), pl.BlockSpec(blk, plane(1)),
            pl.BlockSpec(blk, plane(2)), pl.BlockSpec(blk, plane(3)),
            pl.BlockSpec((1, d), lambda i: (0, 0)),
        ],
        out_specs=pl.BlockSpec(blk, row),
        compiler_params=_params("arbitrary"),
        name="moe_combine",
    )(x1, gate2, planes, planes, planes, planes, g_final)


def _pack_w_in(w):
    sizes = (256, 256, 256, 128, 128, 256, GLA_GATE_RANK, 256, 256, 256, 256, 256, 256, 128, 128)
    offs = np.concatenate([[0], np.cumsum(sizes)])
    mq, mk, mv, gq, gk, gv, ga, gr, rq, rk, rv, rg, sq, sk, sv = (w[:, int(offs[n]):int(offs[n + 1])] for n in range(15))
    merge = w[:, int(offs[15]):]
    d = w.shape[0]
    rep = lambda kv: jnp.repeat(kv.reshape(d, SWA_KV_HEADS, HEAD_DIM), BRANCH_HEADS // SWA_KV_HEADS, axis=1).reshape(d, BRANCH_WIDTH)
    scale = HEAD_DIM ** -0.5
    ga_pad = jnp.pad(ga, ((0, 0), (0, LANES - GLA_GATE_RANK)))
    packed = jnp.concatenate([mq * scale, mk, mv, gq, gk, gv, ga_pad, gr, rq, rk * scale, rv, rg,
                              sq * scale, rep(sk), rep(sv)], axis=1)
    return packed.astype(BF16), merge.astype(BF16)


def kernel(x, c, w_ada, b_ada, g_norm_mix, w_in, w_gla_gate, b_gla_gate, g_gla_norm, g_ret_norm, attn_sinks,
           w_branch, w_out, g_norm_ffn, w_router, b_router, w_gate_up, b_gate_up, w_down, b_down, g_final):
    batch, seq, d = x.shape
    depth = w_ada.shape[0]
    t = batch * seq
    n_alibi = 2 * BRANCH_HEADS
    slopes = [2.0 ** (-(k + 1.0) * (8.0 / n_alibi)) for k in range(n_alibi)]
    swa_slopes, moba_slopes = tuple(slopes[:BRANCH_HEADS]), tuple(slopes[BRANCH_HEADS:])
    log_gamma = jnp.log(1.0 - 2.0 ** (-RET_DECAY_BASE - jnp.arange(BRANCH_HEADS, dtype=F32)))
    log_gamma_row = jnp.repeat(log_gamma, HEAD_DIM).reshape(1, BRANCH_WIDTH)
    dummy_row = jnp.zeros((1, BRANCH_HEADS * GLA_KEY_DIM), F32)
    n_tiles = (TOP_K * t + N_EXPERTS * (TM_EXPERT - 1) + TM_EXPERT - 1) // TM_EXPERT

    mod = _modulation(c, w_ada, b_ada)
    xf = x.reshape(t, d)
    for l in range(depth):
        shift1, scale1, gate1, shift2, scale2, gate2 = (
            mod[l, :, n * d:(n + 1) * d].reshape(batch, 1, d) for n in range(6))
        w_packed, w_merge = _pack_w_in(w_in[l])
        wg = jnp.pad(w_gla_gate[l], ((0, LANES - GLA_GATE_RANK), (0, 0)))
        moba_qkv, gla_f, gla_v, ret_f, ret_v, swa_qkv, kmean = _input_projection(
            xf, scale1, shift1, g_norm_mix[l].reshape(1, d), w_packed, wg, b_gla_gate[l].reshape(1, -1), seq)
        kmean = kmean.reshape(batch, seq // MOBA_BLOCK, BRANCH_WIDTH)

        y_moba = _moba(moba_qkv, kmean, batch, seq, moba_slopes)
        y_gla = _linear_attention(gla_f, gla_v, g_gla_norm[l].reshape(1, -1), dummy_row, batch, seq,
                                  dk=GLA_KEY_DIM, gated=True, qoff=0, koff=128, laoff=256, roff=384,
                                  norm="rms", name="gla")
        y_ret = _linear_attention(ret_f, ret_v, g_ret_norm[l].reshape(1, -1), log_gamma_row, batch, seq,
                                  dk=HEAD_DIM, gated=False, qoff=0, koff=256, laoff=0, roff=512,
                                  norm="group", name="retention")
        sinks_row = jnp.pad(attn_sinks[l].reshape(1, -1), ((0, 0), (0, LANES - BRANCH_HEADS)))
        y_swa = _swa(swa_qkv, sinks_row, batch, seq, swa_slopes)

        wr = jnp.pad(w_router[l], ((0, 0), (0, LANES - N_EXPERTS)))
        br = jnp.pad(b_router[l].reshape(1, -1), ((0, 0), (0, LANES - N_EXPERTS)), constant_values=NEG_BIG)
        x1, h2, top_i, top_w = _merge(
            xf, scale1, shift1, gate1, g_norm_mix[l].reshape(1, d), (y_moba, y_gla, y_ret, y_swa),
            w_merge, w_branch[l].astype(BF16), w_out[l].astype(BF16),
            g_norm_ffn[l].reshape(1, d), scale2, shift2, wr, br, seq)

        tile_e, n_used, src, dst, w_slot = _route(top_i[:, :TOP_K], top_w[:, :TOP_K], t, n_tiles)
        planes = _moe(h2, tile_e, n_used, src, dst, w_slot,
                      w_gate_up[l], b_gate_up[l].reshape(N_EXPERTS, 1, -1),
                      w_down[l], b_down[l].reshape(N_EXPERTS, 1, -1))
        xf = _combine(x1, gate2, planes, g_final.reshape(1, d), seq, final=(l == depth - 1))
    return xf.reshape(batch, seq, d)
```

```python
import functools

import jax
import jax.numpy as jnp
import numpy as np
from jax import lax
from jax.experimental import pallas as pl
from jax.experimental.pallas import tpu as pltpu

F32 = jnp.float32
BF16 = jnp.bfloat16
I32 = jnp.int32
HI = lax.Precision.HIGHEST

HEAD_DIM = 64
N_BRANCHES = 4
BRANCH_HEADS = 4
BRANCH_WIDTH = HEAD_DIM * BRANCH_HEADS
MOBA_BLOCK = 256
MOBA_TOPK = 3
GLA_KEY_DIM = 32
GLA_GATE_RANK = 16
GLA_GATE_TEMP = 16.0
LINEAR_CHUNK = 64
RET_DECAY_BASE = 5.0
SWA_KV_HEADS = 2
WINDOW = 128
N_EXPERTS = 32
TOP_K = 4
SWIGLU_ALPHA = 1.702
SWIGLU_LIMIT = 7.0
NORM_EPS = 1e-5

LANES = 128
TM_IN = 512
TM_MERGE = 256
TM_EXPERT = 256
TM_COMBINE = 512
LIN_GROUP = 256
MOD_COLS = 1536
NEG_BIG = -1e30

_C_MOBA = (0, 768)
_C_GLA = (768, 1664)
_C_RET = (1664, 2688)
_C_SWA = (2688, 3456)
N_PACKED = 3456


def _dot(a, b, precision=None):
    return jnp.dot(a, b, preferred_element_type=F32, precision=precision)


def _dot_nt(a, b, precision=None):
    return lax.dot_general(a, b, (((1,), (1,)), ((), ())), preferred_element_type=F32, precision=precision)


def _dot_tn(a, b, precision=None):
    return lax.dot_general(a, b, (((0,), (0,)), ((), ())), preferred_element_type=F32, precision=precision)


def _norm_mod(x, g, scale, shift):
    ms = jnp.mean(x * x, axis=-1, keepdims=True)
    y = x * lax.rsqrt(ms + NORM_EPS) * g
    return y * (1.0 + scale) + shift


def _log_sigmoid(x):
    return jnp.minimum(x, 0.0) - jnp.log1p(jnp.exp(-jnp.abs(x)))


def _params(*sem):
    return pltpu.CompilerParams(dimension_semantics=tuple(sem))


def _mod_kernel(c_ref, w_ref, b_ref, o_ref):
    c = c_ref[...]
    o_ref[...] = _dot(jax.nn.silu(c), w_ref[...], HI) + b_ref[...]


def _modulation(c, w_ada, b_ada):
    depth, d, n = w_ada.shape
    b = c.shape[0]
    return pl.pallas_call(
        _mod_kernel,
        out_shape=jax.ShapeDtypeStruct((depth, b, n), F32),
        grid=(depth, n // MOD_COLS),
        in_specs=[
            pl.BlockSpec((b, d), lambda l, j: (0, 0)),
            pl.BlockSpec((None, d, MOD_COLS), lambda l, j: (l, 0, j)),
            pl.BlockSpec((None, 1, MOD_COLS), lambda l, j: (l, 0, j)),
        ],
        out_specs=pl.BlockSpec((None, b, MOD_COLS), lambda l, j: (l, 0, j)),
        compiler_params=_params("arbitrary", "arbitrary"),
        name="adaln_mod",
    )(c, w_ada, b_ada.reshape(depth, 1, n))


def _in_kernel(x_ref, sc_ref, sh_ref, g_ref, w_ref, wg_ref, bg_ref,
               moba_ref, glaf_ref, glav_ref, retf_ref, retv_ref, swa_ref, kmean_ref):
    h = _norm_mod(x_ref[...], g_ref[...], sc_ref[...], sh_ref[...]).astype(BF16)

    zm = _dot(h, w_ref[:, _C_MOBA[0]:_C_MOBA[1]])
    moba_ref[...] = zm.astype(BF16)
    for blk in range(TM_IN // MOBA_BLOCK):
        kb = zm[blk * MOBA_BLOCK:(blk + 1) * MOBA_BLOCK, 256:512]
        kmean_ref[blk:blk + 1, :] = jnp.mean(kb, axis=0, keepdims=True)

    zg = _dot(h, w_ref[:, _C_GLA[0]:_C_GLA[1]])
    gate_logit = _dot(zg[:, 512:640], wg_ref[...], HI) + bg_ref[...]
    glaf_ref[:, 0:128] = zg[:, 0:128] * (GLA_KEY_DIM ** -0.5)
    glaf_ref[:, 128:256] = zg[:, 128:256]
    glaf_ref[:, 256:384] = _log_sigmoid(gate_logit) / GLA_GATE_TEMP
    glaf_ref[:, 384:640] = zg[:, 640:896]
    glav_ref[...] = zg[:, 256:512].astype(BF16)

    zr = _dot(h, w_ref[:, _C_RET[0]:_C_RET[1]])
    retf_ref[:, 0:512] = zr[:, 0:512]
    retf_ref[:, 512:768] = zr[:, 768:1024]
    retv_ref[...] = zr[:, 512:768].astype(BF16)

    swa_ref[...] = _dot(h, w_ref[:, _C_SWA[0]:_C_SWA[1]]).astype(BF16)


def _input_projection(x2, scale, shift, g, w_packed, wg, bg, seq):
    t, d = x2.shape
    tiles_per_seq = seq // TM_IN
    nblk = TM_IN // MOBA_BLOCK
    row = lambda i: (i, 0)
    per_batch = lambda i: (i // tiles_per_seq, 0, 0)
    const2 = lambda i: (0, 0)
    outs = pl.pallas_call(
        _in_kernel,
        out_shape=(
            jax.ShapeDtypeStruct((t, 768), BF16),
            jax.ShapeDtypeStruct((t, 640), F32),
            jax.ShapeDtypeStruct((t, 256), BF16),
            jax.ShapeDtypeStruct((t, 768), F32),
            jax.ShapeDtypeStruct((t, 256), BF16),
            jax.ShapeDtypeStruct((t, 768), BF16),
            jax.ShapeDtypeStruct((t // TM_IN, nblk, 256), F32),
        ),
        grid=(t // TM_IN,),
        in_specs=[
            pl.BlockSpec((TM_IN, d), row),
            pl.BlockSpec((None, 1, d), per_batch),
            pl.BlockSpec((None, 1, d), per_batch),
            pl.BlockSpec((1, d), const2),
            pl.BlockSpec((d, N_PACKED), const2),
            pl.BlockSpec((LANES, LANES), const2),
            pl.BlockSpec((1, LANES), const2),
        ],
        out_specs=(
            pl.BlockSpec((TM_IN, 768), row),
            pl.BlockSpec((TM_IN, 640), row),
            pl.BlockSpec((TM_IN, 256), row),
            pl.BlockSpec((TM_IN, 768), row),
            pl.BlockSpec((TM_IN, 256), row),
            pl.BlockSpec((TM_IN, 768), row),
            pl.BlockSpec((None, nblk, 256), lambda i: (i, 0, 0)),
        ),
        compiler_params=_params("arbitrary"),
        name="input_proj",
    )(x2, scale, shift, g, w_packed, wg, bg)
    return outs


def _moba_kernel(q_ref, k_ref, v_ref, km_ref, o_ref, m_sc, l_sc, acc_sc, *, slopes):
    i = pl.program_id(1)
    nb = MOBA_BLOCK
    n_blocks = km_ref.shape[0]
    q = q_ref[...]
    lane = lax.broadcasted_iota(I32, (1, BRANCH_WIDTH), 1)
    head_of_lane = lane // HEAD_DIM

    rows = lax.broadcasted_iota(I32, (LANES, BRANCH_WIDTH), 0)
    lanes_full = lax.broadcasted_iota(I32, (LANES, BRANCH_WIDTH), 1)
    km = km_ref[...]
    km_t = jnp.concatenate([km] * (LANES // n_blocks), axis=0)
    km_heads = jnp.where((rows // n_blocks) == (lanes_full // HEAD_DIM), km_t, 0.0)
    gate_t = _dot_nt(km_heads, q.astype(F32), HI)

    jrow = lax.broadcasted_iota(I32, (n_blocks, nb), 0)
    valid = jrow < i
    sel_rows = []
    for h in range(BRANCH_HEADS):
        g = jnp.where(valid, gate_t[h * n_blocks:(h + 1) * n_blocks, :], -jnp.inf)
        cnt = jnp.zeros((n_blocks, nb), F32)
        for jp in range(n_blocks):
            gj = g[jp:jp + 1, :]
            beats = (gj > g) | ((gj == g) & (jp < jrow))
            cnt = cnt + beats.astype(F32)
        sel_rows.append(((cnt < MOBA_TOPK) & valid).astype(F32))
    sel_rows.append(jnp.zeros((LANES - BRANCH_HEADS * n_blocks, nb), F32))
    sel_t = jnp.concatenate(sel_rows, axis=0).T

    r_io = lax.broadcasted_iota(I32, (nb, nb), 0)
    c_io = lax.broadcasted_iota(I32, (nb, nb), 1)
    rel = (r_io - c_io).astype(F32)
    lane128 = lax.broadcasted_iota(I32, (nb, LANES), 1)

    m_sc[...] = jnp.full(m_sc.shape, -jnp.inf, F32)
    l_sc[...] = jnp.zeros(l_sc.shape, F32)
    acc_sc[...] = jnp.zeros(acc_sc.shape, F32)

    def process(j, own):
        start = pl.multiple_of(j * nb, nb)
        kj = k_ref[pl.ds(start, nb), :]
        vj = v_ref[pl.ds(start, nb), :]
        dist = rel + ((i - j) * nb).astype(F32)
        for h in range(BRANCH_HEADS):
            hm = head_of_lane == h
            s = _dot_nt(jnp.where(hm, q, jnp.zeros_like(q)), kj) - slopes[h] * dist
            if own:
                mask = rel >= 0.0
            else:
                col = jnp.sum(jnp.where(lane128 == h * n_blocks + j, sel_t, 0.0), axis=1, keepdims=True)
                mask = col > 0.5
            s = jnp.where(mask, s, -jnp.inf)
            m_old = m_sc[h]
            m_new = jnp.maximum(m_old, jnp.max(s, axis=1, keepdims=True))
            alpha = jnp.exp(m_old - m_new)
            p = jnp.exp(s - m_new)
            l_sc[h] = alpha * l_sc[h] + jnp.sum(p, axis=1, keepdims=True)
            m_sc[h] = m_new
            pv = _dot(p.astype(BF16), jnp.where(hm, vj, jnp.zeros_like(vj)))
            acc_sc[...] = acc_sc[...] * jnp.where(hm, alpha, 1.0) + pv

    process(i, True)

    def body(j, carry):
        process(j, False)
        return carry

    lax.fori_loop(0, i, body, 0)

    inv = jnp.zeros((nb, BRANCH_WIDTH), F32)
    for h in range(BRANCH_HEADS):
        inv = jnp.where(head_of_lane == h, 1.0 / l_sc[h], inv)
    o_ref[...] = (acc_sc[...] * inv).astype(BF16)


def _moba(qkv, kmean, batch, seq, slopes):
    t = qkv.shape[0]
    nq = seq // MOBA_BLOCK
    kern = functools.partial(_moba_kernel, slopes=slopes)
    return pl.pallas_call(
        kern,
        out_shape=jax.ShapeDtypeStruct((t, BRANCH_WIDTH), BF16),
        grid=(batch, nq),
        in_specs=[
            pl.BlockSpec((MOBA_BLOCK, BRANCH_WIDTH), lambda b, i: (b * nq + i, 0)),
            pl.BlockSpec((seq, BRANCH_WIDTH), lambda b, i: (b, 1)),
            pl.BlockSpec((seq, BRANCH_WIDTH), lambda b, i: (b, 2)),
            pl.BlockSpec((None, nq, BRANCH_WIDTH), lambda b, i: (b, 0, 0)),
        ],
        out_specs=pl.BlockSpec((MOBA_BLOCK, BRANCH_WIDTH), lambda b, i: (b * nq + i, 0)),
        scratch_shapes=[
            pltpu.VMEM((BRANCH_HEADS, MOBA_BLOCK, 1), F32),
            pltpu.VMEM((BRANCH_HEADS, MOBA_BLOCK, 1), F32),
            pltpu.VMEM((MOBA_BLOCK, BRANCH_WIDTH), F32),
        ],
        compiler_params=_params("arbitrary", "arbitrary"),
        name="moba_attn",
    )(qkv, qkv, qkv, kmean)


def _swa_kernel(q_ref, kp_ref, kc_ref, vp_ref, vc_ref, sink_ref, o_ref, *, slopes):
    n = pl.program_id(1)
    w = WINDOW
    q = q_ref[...]
    kp, kc, vp, vc = kp_ref[...], kc_ref[...], vp_ref[...], vc_ref[...]
    head_of_lane = lax.broadcasted_iota(I32, (1, BRANCH_WIDTH), 1) // HEAD_DIM
    r_io = lax.broadcasted_iota(I32, (w, w), 0)
    c_io = lax.broadcasted_iota(I32, (w, w), 1)
    rel = (r_io - c_io).astype(F32)
    allow_cur = rel >= 0.0
    allow_prev = (rel < 0.0) & (n > 0)
    out = jnp.zeros((w, BRANCH_WIDTH), F32)
    zq = jnp.zeros_like(q)
    zv = jnp.zeros_like(vp)
    for h in range(BRANCH_HEADS):
        hm = head_of_lane == h
        qh = jnp.where(hm, q, zq)
        sp = jnp.where(allow_prev, _dot_nt(qh, kp) - slopes[h] * (rel + float(w)), -jnp.inf)
        sc = jnp.where(allow_cur, _dot_nt(qh, kc) - slopes[h] * rel, -jnp.inf)
        sink = sink_ref[0:1, h:h + 1]
        m = jnp.maximum(jnp.maximum(jnp.max(sp, axis=1, keepdims=True), jnp.max(sc, axis=1, keepdims=True)), sink)
        pp = jnp.exp(sp - m)
        pc = jnp.exp(sc - m)
        l = jnp.sum(pp, axis=1, keepdims=True) + jnp.sum(pc, axis=1, keepdims=True) + jnp.exp(sink - m)
        o = _dot(pp.astype(BF16), jnp.where(hm, vp, zv)) + _dot(pc.astype(BF16), jnp.where(hm, vc, zv))
        out = out + o * (1.0 / l)
    o_ref[...] = out.astype(BF16)


def _swa(qkv, sinks_row, batch, seq, slopes):
    t = qkv.shape[0]
    nq = seq // WINDOW
    kern = functools.partial(_swa_kernel, slopes=slopes)
    cur = lambda col: (lambda b, n: (b * nq + n, col))
    prev = lambda col: (lambda b, n: (b * nq + jnp.maximum(n - 1, 0), col))
    blk = (WINDOW, BRANCH_WIDTH)
    return pl.pallas_call(
        kern,
        out_shape=jax.ShapeDtypeStruct((t, BRANCH_WIDTH), BF16),
        grid=(batch, nq),
        in_specs=[
            pl.BlockSpec(blk, cur(0)),
            pl.BlockSpec(blk, prev(1)),
            pl.BlockSpec(blk, cur(1)),
            pl.BlockSpec(blk, prev(2)),
            pl.BlockSpec(blk, cur(2)),
            pl.BlockSpec((1, LANES), lambda b, n: (0, 0)),
        ],
        out_specs=pl.BlockSpec(blk, cur(0)),
        compiler_params=_params("arbitrary", "arbitrary"),
        name="swa_attn",
    )(qkv, qkv, qkv, qkv, qkv, sinks_row)


def _lin_kernel(f_ref, v_ref, g_ref, lg_ref, o_ref, st_sc, *, dk, gated, qoff, koff, laoff, roff, norm):
    width = BRANCH_HEADS * dk
    grp = LIN_GROUP
    ch = LINEAR_CHUNK
    seq = f_ref.shape[0]

    r_io = lax.broadcasted_iota(I32, (grp, grp), 0)
    c_io = lax.broadcasted_iota(I32, (grp, grp), 1)
    same = (r_io // ch) == (c_io // ch)
    ltri = (same & (c_io <= r_io)).astype(F32)
    lfull = same.astype(F32)
    head_avg = ((r_io // HEAD_DIM) == (c_io // HEAD_DIM)).astype(F32) * (1.0 / HEAD_DIM)
    lane_k = lax.broadcasted_iota(I32, (1, width), 1) // dk
    lane_v = lax.broadcasted_iota(I32, (1, BRANCH_WIDTH), 1) // HEAD_DIM
    bd_mask = (lax.broadcasted_iota(I32, (BRANCH_WIDTH, width), 0) // HEAD_DIM
               == lax.broadcasted_iota(I32, (BRANCH_WIDTH, width), 1) // dk)
    gain = g_ref[...]

    st_sc[...] = jnp.zeros(st_sc.shape, F32)

    def group(gi, carry):
        r0 = pl.multiple_of(gi * grp, grp)
        rows = pl.ds(r0, grp)
        q = f_ref[rows, qoff:qoff + width]
        k = f_ref[rows, koff:koff + width]
        if gated:
            la = f_ref[rows, laoff:laoff + width]
        else:
            la = jnp.broadcast_to(lg_ref[...], (grp, width))
        b = _dot(ltri, la, HI)
        be = _dot(lfull, la, HI)
        qd = (q * jnp.exp(b)).astype(BF16)
        kin = (k * jnp.exp(-b)).astype(BF16)
        kend = (k * jnp.exp(be - b)).astype(BF16)
        dec = jnp.exp(be)
        v = v_ref[rows, :]
        zq = jnp.zeros_like(qd)
        zv = jnp.zeros_like(v)

        o = jnp.zeros((grp, BRANCH_WIDTH), F32)
        for h in range(BRANCH_HEADS):
            a = _dot_nt(jnp.where(lane_k == h, qd, zq), kin) * ltri
            o = o + _dot(a.astype(BF16), jnp.where(lane_v == h, v, zv))

        st = st_sc[...]
        parts = []
        for c in range(grp // ch):
            sl = slice(c * ch, (c + 1) * ch)
            parts.append(_dot_nt(qd[sl], st.astype(BF16)))
            kv_t = _dot_tn(v[sl], kend[sl])
            st = st * dec[c * ch:c * ch + 1, :] + jnp.where(bd_mask, kv_t, 0.0)
        st_sc[...] = st
        o = o + jnp.concatenate(parts, axis=0)

        if norm == "rms":
            ms = _dot(o * o, head_avg, HI)
            y = o * lax.rsqrt(ms + NORM_EPS) * gain
        else:
            mu = _dot(o, head_avg, HI)
            xc = o - mu
            var = _dot(xc * xc, head_avg, HI)
            y = xc * lax.rsqrt(var + NORM_EPS) * gain
        r = f_ref[rows, roff:roff + BRANCH_WIDTH]
        o_ref[rows, :] = (y * jax.nn.silu(r)).astype(BF16)
        return carry

    lax.fori_loop(0, seq // grp, group, 0)


def _linear_attention(feat, v, gain, log_gamma_row, batch, seq, *, dk, gated, qoff, koff, laoff, roff, norm, name):
    t, fw = feat.shape
    width = BRANCH_HEADS * dk
    kern = functools.partial(_lin_kernel, dk=dk, gated=gated, qoff=qoff, koff=koff, laoff=laoff, roff=roff, norm=norm)
    return pl.pallas_call(
        kern,
        out_shape=jax.ShapeDtypeStruct((t, BRANCH_WIDTH), BF16),
        grid=(batch,),
        in_specs=[
            pl.BlockSpec((seq, fw), lambda b: (b, 0)),
            pl.BlockSpec((seq, BRANCH_WIDTH), lambda b: (b, 0)),
            pl.BlockSpec((1, BRANCH_WIDTH), lambda b: (0, 0)),
            pl.BlockSpec((1, width), lambda b: (0, 0)),
        ],
        out_specs=pl.BlockSpec((seq, BRANCH_WIDTH), lambda b: (b, 0)),
        scratch_shapes=[pltpu.VMEM((BRANCH_WIDTH, width), F32)],
        compiler_params=_params("arbitrary"),
        name=name,
    )(feat, v, gain, log_gamma_row)


def _merge_kernel(x_ref, sc1_ref, sh1_ref, gt1_ref, gnm_ref, ym_ref, yg_ref, yr_ref, ys_ref,
                  wm_ref, wb_ref, wo_ref, gnf_ref, sc2_ref, sh2_ref, wr_ref, br_ref,
                  x1_ref, h2_ref, ti_ref, tw_ref):
    x = x_ref[...]
    d = x.shape[1]
    hb = _norm_mod(x, gnm_ref[...], sc1_ref[...], sh1_ref[...]).astype(BF16)
    mixed = jnp.zeros(x.shape, F32)
    for n, y_ref in enumerate((ym_ref, yg_ref, yr_ref, ys_ref)):
        gate = jax.nn.sigmoid(_dot(hb, wm_ref[:, n * d:(n + 1) * d]))
        mixed = mixed + gate * _dot(y_ref[...], wb_ref[n])
    x1 = x + gt1_ref[...] * _dot(mixed.astype(BF16), wo_ref[...])
    x1_ref[...] = x1

    h2 = _norm_mod(x1, gnf_ref[...], sc2_ref[...], sh2_ref[...])
    sub = d // LANES
    for c in range(sub):
        h2_ref[pl.ds(c, x.shape[0], stride=sub), :] = h2[:, c * LANES:(c + 1) * LANES]
    logits = _dot(h2, wr_ref[...], HI) + br_ref[...]
    lane = lax.broadcasted_iota(I32, logits.shape, 1)
    vals, idxs = [], []
    cur = logits
    for _ in range(TOP_K):
        m = jnp.max(cur, axis=1, keepdims=True)
        idx = jnp.min(jnp.where(cur == m, lane, LANES), axis=1, keepdims=True)
        vals.append(m)
        idxs.append(idx)
        cur = jnp.where(lane == idx, -jnp.inf, cur)
    es = [jnp.exp(v - vals[0]) for v in vals]
    tot = es[0] + es[1] + es[2] + es[3]
    ti = jnp.zeros(logits.shape, I32)
    tw = jnp.zeros(logits.shape, F32)
    for k in range(TOP_K):
        ti = jnp.where(lane == k, idxs[k], ti)
        tw = jnp.where(lane == k, es[k] / tot, tw)
    ti_ref[...] = ti
    tw_ref[...] = tw


def _merge(x2, sc1, sh1, gt1, gnm, ys, wm, wb, wo, gnf, sc2, sh2, wr, br, seq):
    t, d = x2.shape
    tiles_per_seq = seq // TM_MERGE
    row = lambda i: (i, 0)
    per_batch = lambda i: (i // tiles_per_seq, 0, 0)
    c2 = lambda i: (0, 0)
    c3 = lambda i: (0, 0, 0)
    vec = pl.BlockSpec((None, 1, d), per_batch)
    ytile = pl.BlockSpec((TM_MERGE, BRANCH_WIDTH), row)
    return pl.pallas_call(
        _merge_kernel,
        out_shape=(
            jax.ShapeDtypeStruct((t, d), F32),
            jax.ShapeDtypeStruct((t * (d // LANES), LANES), F32),
            jax.ShapeDtypeStruct((t, LANES), I32),
            jax.ShapeDtypeStruct((t, LANES), F32),
        ),
        grid=(t // TM_MERGE,),
        in_specs=[
            pl.BlockSpec((TM_MERGE, d), row), vec, vec, vec,
            pl.BlockSpec((1, d), c2),
            ytile, ytile, ytile, ytile,
            pl.BlockSpec((d, N_BRANCHES * d), c2),
            pl.BlockSpec((N_BRANCHES, BRANCH_WIDTH, d), c3),
            pl.BlockSpec((d, d), c2),
            pl.BlockSpec((1, d), c2), vec, vec,
            pl.BlockSpec((d, LANES), c2),
            pl.BlockSpec((1, LANES), c2),
        ],
        out_specs=(
            pl.BlockSpec((TM_MERGE, d), row),
            pl.BlockSpec((TM_MERGE * (d // LANES), LANES), row),
            pl.BlockSpec((TM_MERGE, LANES), row),
            pl.BlockSpec((TM_MERGE, LANES), row),
        ),
        compiler_params=_params("arbitrary"),
        name="merge_router",
    )(x2, sc1, sh1, gt1, gnm, *ys, wm, wb, wo, gnf, sc2, sh2, wr, br)


def _route(top_i, top_w, n_tok, n_tiles):
    tm = TM_EXPERT
    flat_e = top_i.reshape(-1)
    flat_w = top_w.reshape(-1)
    n_assign = flat_e.shape[0]
    sorted_e, order, sorted_w = lax.sort((flat_e, jnp.arange(n_assign, dtype=I32), flat_w), num_keys=1, is_stable=True)
    bounds = jnp.searchsorted(sorted_e, jnp.arange(N_EXPERTS + 1, dtype=I32)).astype(I32)
    counts = bounds[1:] - bounds[:-1]
    padded = ((counts + tm - 1) // tm) * tm
    gend = jnp.cumsum(padded)
    gstart = gend - padded
    tile_start = jnp.arange(n_tiles, dtype=I32) * tm
    tile_e_raw = jnp.sum((tile_start[:, None] >= gend[None, :]).astype(I32), axis=1)
    tile_e = jnp.minimum(tile_e_raw, N_EXPERTS - 1).astype(I32)
    n_used = (gend[-1] // tm).astype(I32).reshape(1)
    t_off = tile_start - gstart[tile_e]
    t_cnt = jnp.where(tile_e_raw < N_EXPERTS, counts[tile_e], 0)
    row = jnp.arange(tm, dtype=I32)[None, :]
    j = t_off[:, None] + row
    valid = j < t_cnt[:, None]
    p = jnp.clip(bounds[tile_e][:, None] + j, 0, n_assign - 1)
    a = order[p]
    tok = a // TOP_K
    kk = a % TOP_K
    src = jnp.where(valid, tok, 0).astype(I32)
    dst = jnp.where(valid, kk * n_tok + tok, TOP_K * n_tok + row).astype(I32)
    w = jnp.where(valid, sorted_w[p], 0.0).astype(F32)
    return tile_e, n_used, src.reshape(-1), dst.reshape(-1), w.reshape(-1, 1)


def _moe_kernel(te_ref, nu_ref, src_hbm, dst_hbm, w_ref, h_hbm, wgu_ref, bgu_ref, wd_ref, bd_ref,
                y_hbm, src_s, dst_s, xs, ys, wgu_b, wd_b, isem, gsem, ssem):
    i = pl.program_id(0)
    tm = TM_EXPERT
    sub = xs.shape[1] // tm
    d = sub * LANES
    n_used = nu_ref[0]

    def tok_rows(t):
        return pl.ds(pl.multiple_of(t * sub, sub), sub)

    def idx_copies(tile, buf):
        base = pl.multiple_of(tile * tm, tm)
        return (pltpu.make_async_copy(src_hbm.at[pl.ds(base, tm)], src_s.at[buf], isem.at[buf, 0]),
                pltpu.make_async_copy(dst_hbm.at[pl.ds(base, tm)], dst_s.at[buf], isem.at[buf, 1]))

    def start_idx(tile, buf):
        for cp in idx_copies(tile, buf):
            cp.start()

    def wait_idx(tile, buf):
        for cp in idx_copies(tile, buf):
            cp.wait()

    def issue_gather(buf, slot):
        def body(r, carry):
            pltpu.make_async_copy(h_hbm.at[tok_rows(src_s[buf, r])], xs.at[slot, tok_rows(r)], gsem.at[slot]).start()
            return carry
        lax.fori_loop(0, tm, body, 0, unroll=8)

    def wait_gather(slot):
        pltpu.make_async_copy(h_hbm.at[pl.ds(0, tm * sub)], xs.at[slot], gsem.at[slot]).wait()

    def wait_scatter():
        pltpu.make_async_copy(ys, y_hbm.at[pl.ds(0, tm * sub)], ssem.at[0]).wait()

    @pl.when(i == 0)
    def _():
        ys[...] = jnp.zeros(ys.shape, F32)
        fill = pltpu.make_async_copy(ys, y_hbm.at[pl.ds(y_hbm.shape[0] - tm * sub, tm * sub)], ssem.at[0])
        fill.start()
        fill.wait()

        @pl.when(n_used > 0)
        def _():
            start_idx(0, 0)
            wait_idx(0, 0)
            issue_gather(0, 0)

        @pl.when(n_used > 1)
        def _():
            start_idx(1, 1)

    @pl.when(i < n_used)
    def _():
        slot = lax.rem(i, 2)
        buf = lax.rem(i, 3)

        @pl.when(i + 1 < n_used)
        def _():
            nbuf = lax.rem(i + 1, 3)
            wait_idx(i + 1, nbuf)
            issue_gather(nbuf, 1 - slot)

        @pl.when(i + 2 < n_used)
        def _():
            start_idx(i + 2, lax.rem(i + 2, 3))

        @pl.when((i == 0) | (te_ref[i] != te_ref[jnp.maximum(i - 1, 0)]))
        def _():
            rows = d // 8

            def cast(c, carry):
                r = pl.ds(pl.multiple_of(c * rows, rows), rows)
                wgu_b[r, :] = wgu_ref[r, :].astype(BF16)
                wd_b[r, :] = wd_ref[r, :].astype(BF16)
                return carry
            lax.fori_loop(0, 8, cast, 0)

        wait_gather(slot)
        x = jnp.concatenate([xs[slot, pl.ds(c, tm, stride=sub), :] for c in range(sub)], axis=1).astype(BF16)
        gu = _dot(x, wgu_b[...]) + bgu_ref[...]
        x_glu = jnp.minimum(gu[:, :d], SWIGLU_LIMIT)
        x_lin = jnp.clip(gu[:, d:], -SWIGLU_LIMIT, SWIGLU_LIMIT)
        act = x_glu * jax.nn.sigmoid(SWIGLU_ALPHA * x_glu) * (x_lin + 1.0)
        out = (_dot(act.astype(BF16), wd_b[...]) + bd_ref[...]) * w_ref[...]

        @pl.when(i > 0)
        def _():
            wait_scatter()

        for c in range(sub):
            ys[pl.ds(c, tm, stride=sub), :] = out[:, c * LANES:(c + 1) * LANES]

        def scatter(r, carry):
            pltpu.make_async_copy(ys.at[tok_rows(r)], y_hbm.at[tok_rows(dst_s[buf, r])], ssem.at[0]).start()
            return carry
        lax.fori_loop(0, tm, scatter, 0, unroll=8)

        @pl.when(i == n_used - 1)
        def _():
            wait_scatter()


def _moe(h2_rows, tile_e, n_used, src, dst, w_slot, wgu, bgu, wd, bd, layer):
    d = wgu.shape[2]
    sub = d // LANES
    t = h2_rows.shape[0] // sub
    n_exp = wgu.shape[1]
    n_tiles = tile_e.shape[0]
    tm = TM_EXPERT
    any_spec = pl.BlockSpec(memory_space=pl.ANY)
    expert = lambda i, te, nu: (layer, te[i], 0, 0)
    grid_spec = pltpu.PrefetchScalarGridSpec(
        num_scalar_prefetch=2,
        grid=(n_tiles,),
        in_specs=[
            any_spec, any_spec,
            pl.BlockSpec((tm, 1), lambda i, te, nu: (i, 0)),
            any_spec,
            pl.BlockSpec((None, None, d, 2 * d), expert),
            pl.BlockSpec((None, None, 1, 2 * d), expert),
            pl.BlockSpec((None, None, d, d), expert),
            pl.BlockSpec((None, None, 1, d), expert),
        ],
        out_specs=any_spec,
        scratch_shapes=[
            pltpu.SMEM((3, tm), I32),
            pltpu.SMEM((3, tm), I32),
            pltpu.VMEM((2, tm * sub, LANES), F32),
            pltpu.VMEM((tm * sub, LANES), F32),
            pltpu.VMEM((d, 2 * d), BF16),
            pltpu.VMEM((d, d), BF16),
            pltpu.SemaphoreType.DMA((3, 2)),
            pltpu.SemaphoreType.DMA((2,)),
            pltpu.SemaphoreType.DMA((1,)),
        ],
    )
    return pl.pallas_call(
        _moe_kernel,
        out_shape=jax.ShapeDtypeStruct(((TOP_K * t + tm) * sub, LANES), F32),
        grid_spec=grid_spec,
        compiler_params=pltpu.CompilerParams(dimension_semantics=("arbitrary",), has_side_effects=True),
        name="moe_grouped",
    )(tile_e, n_used, src, dst, w_slot, h2_rows, wgu, bgu.reshape(-1, n_exp, 1, 2 * d), wd, bd.reshape(-1, n_exp, 1, d))


def _combine_kernel(x_ref, gt_ref, y0_ref, y1_ref, y2_ref, y3_ref, gf_ref, o_ref, y_sc, *, final):
    tm, d = x_ref.shape
    sub = d // LANES
    y = (y0_ref[...] + y1_ref[...]) + (y2_ref[...] + y3_ref[...])
    y_sc[...] = y
    y = jnp.concatenate([y_sc[pl.ds(c, tm, stride=sub), :] for c in range(sub)], axis=1)
    x = x_ref[...] + gt_ref[...] * y
    if final:
        ms = jnp.mean(x * x, axis=-1, keepdims=True)
        x = x * lax.rsqrt(ms + NORM_EPS) * gf_ref[...]
    o_ref[...] = x


def _combine(x1, gate2, planes, g_final, seq, final):
    t, d = x1.shape
    tiles_per_seq = seq // TM_COMBINE
    nt = t // TM_COMBINE
    row = lambda i: (i, 0)
    plane = lambda k: (lambda i: (k * nt + i, 0))
    blk = (TM_COMBINE, d)
    pblk = (TM_COMBINE * (d // LANES), LANES)
    return pl.pallas_call(
        functools.partial(_combine_kernel, final=final),
        out_shape=jax.ShapeDtypeStruct((t, d), F32),
        grid=(nt,),
        in_specs=[
            pl.BlockSpec(blk, row),
            pl.BlockSpec((None, 1, d), lambda i: (i // tiles_per_seq, 0, 0)),
            pl.BlockSpec(pblk, plane(0)), pl.BlockSpec(pblk, plane(1)),
            pl.BlockSpec(pblk, plane(2)), pl.BlockSpec(pblk, plane(3)),
            pl.BlockSpec((1, d), lambda i: (0, 0)),
        ],
        out_specs=pl.BlockSpec(blk, row),
        scratch_shapes=[pltpu.VMEM(pblk, F32)],
        compiler_params=_params("arbitrary"),
        name="moe_combine",
    )(x1, gate2, planes, planes, planes, planes, g_final)


def _pack_w_in(w):
    sizes = (256, 256, 256, 128, 128, 256, GLA_GATE_RANK, 256, 256, 256, 256, 256, 256, 128, 128)
    offs = np.concatenate([[0], np.cumsum(sizes)])
    mq, mk, mv, gq, gk, gv, ga, gr, rq, rk, rv, rg, sq, sk, sv = (w[:, int(offs[n]):int(offs[n + 1])] for n in range(15))
    merge = w[:, int(offs[15]):]
    d = w.shape[0]
    rep = lambda kv: jnp.repeat(kv.reshape(d, SWA_KV_HEADS, HEAD_DIM), BRANCH_HEADS // SWA_KV_HEADS, axis=1).reshape(d, BRANCH_WIDTH)
    scale = HEAD_DIM ** -0.5
    ga_pad = jnp.pad(ga, ((0, 0), (0, LANES - GLA_GATE_RANK)))
    packed = jnp.concatenate([mq * scale, mk, mv, gq, gk, gv, ga_pad, gr, rq, rk * scale, rv, rg,
                              sq * scale, rep(sk), rep(sv)], axis=1)
    return packed.astype(BF16), merge.astype(BF16)


def kernel(x, c, w_ada, b_ada, g_norm_mix, w_in, w_gla_gate, b_gla_gate, g_gla_norm, g_ret_norm, attn_sinks,
           w_branch, w_out, g_norm_ffn, w_router, b_router, w_gate_up, b_gate_up, w_down, b_down, g_final):
    batch, seq, d = x.shape
    depth = w_ada.shape[0]
    t = batch * seq
    n_alibi = 2 * BRANCH_HEADS
    slopes = [2.0 ** (-(k + 1.0) * (8.0 / n_alibi)) for k in range(n_alibi)]
    swa_slopes, moba_slopes = tuple(slopes[:BRANCH_HEADS]), tuple(slopes[BRANCH_HEADS:])
    log_gamma = jnp.log(1.0 - 2.0 ** (-RET_DECAY_BASE - jnp.arange(BRANCH_HEADS, dtype=F32)))
    log_gamma_row = jnp.repeat(log_gamma, HEAD_DIM).reshape(1, BRANCH_WIDTH)
    dummy_row = jnp.zeros((1, BRANCH_HEADS * GLA_KEY_DIM), F32)
    n_tiles = (TOP_K * t + N_EXPERTS * (TM_EXPERT - 1) + TM_EXPERT - 1) // TM_EXPERT

    mod = _modulation(c, w_ada, b_ada)
    xf = x.reshape(t, d)
    for l in range(depth):
        shift1, scale1, gate1, shift2, scale2, gate2 = (
            mod[l, :, n * d:(n + 1) * d].reshape(batch, 1, d) for n in range(6))
        w_packed, w_merge = _pack_w_in(w_in[l])
        wg = jnp.pad(w_gla_gate[l], ((0, LANES - GLA_GATE_RANK), (0, 0)))
        moba_qkv, gla_f, gla_v, ret_f, ret_v, swa_qkv, kmean = _input_projection(
            xf, scale1, shift1, g_norm_mix[l].reshape(1, d), w_packed, wg, b_gla_gate[l].reshape(1, -1), seq)
        kmean = kmean.reshape(batch, seq // MOBA_BLOCK, BRANCH_WIDTH)

        y_moba = _moba(moba_qkv, kmean, batch, seq, moba_slopes)
        y_gla = _linear_attention(gla_f, gla_v, g_gla_norm[l].reshape(1, -1), dummy_row, batch, seq,
                                  dk=GLA_KEY_DIM, gated=True, qoff=0, koff=128, laoff=256, roff=384,
                                  norm="rms", name="gla")
        y_ret = _linear_attention(ret_f, ret_v, g_ret_norm[l].reshape(1, -1), log_gamma_row, batch, seq,
                                  dk=HEAD_DIM, gated=False, qoff=0, koff=256, laoff=0, roff=512,
                                  norm="group", name="retention")
        sinks_row = jnp.pad(attn_sinks[l].reshape(1, -1), ((0, 0), (0, LANES - BRANCH_HEADS)))
        y_swa = _swa(swa_qkv, sinks_row, batch, seq, swa_slopes)

        wr = jnp.pad(w_router[l], ((0, 0), (0, LANES - N_EXPERTS)))
        br = jnp.pad(b_router[l].reshape(1, -1), ((0, 0), (0, LANES - N_EXPERTS)), constant_values=NEG_BIG)
        x1, h2, top_i, top_w = _merge(
            xf, scale1, shift1, gate1, g_norm_mix[l].reshape(1, d), (y_moba, y_gla, y_ret, y_swa),
            w_merge, w_branch[l].astype(BF16), w_out[l].astype(BF16),
            g_norm_ffn[l].reshape(1, d), scale2, shift2, wr, br, seq)

        tile_e, n_used, src, dst, w_slot = _route(top_i[:, :TOP_K], top_w[:, :TOP_K], t, n_tiles)
        planes = _moe(h2, tile_e, n_used, src, dst, w_slot, w_gate_up, b_gate_up, w_down, b_down, l)
        xf = _combine(x1, gate2, planes, g_final.reshape(1, d), seq, final=(l == depth - 1))
    return xf.reshape(batch, seq, d)
```

```python
import functools

import jax
import jax.numpy as jnp
import numpy as np
from jax import lax
from jax.experimental import pallas as pl
from jax.experimental.pallas import tpu as pltpu

F32 = jnp.float32
BF16 = jnp.bfloat16
I32 = jnp.int32
HI = lax.Precision.HIGHEST

HEAD_DIM = 64
N_BRANCHES = 4
BRANCH_HEADS = 4
BRANCH_WIDTH = HEAD_DIM * BRANCH_HEADS
MOBA_BLOCK = 256
MOBA_TOPK = 3
GLA_KEY_DIM = 32
GLA_GATE_RANK = 16
GLA_GATE_TEMP = 16.0
LINEAR_CHUNK = 64
RET_DECAY_BASE = 5.0
SWA_KV_HEADS = 2
WINDOW = 128
N_EXPERTS = 32
TOP_K = 4
SWIGLU_ALPHA = 1.702
SWIGLU_LIMIT = 7.0
NORM_EPS = 1e-5

LANES = 128
TM_IN = 512
TM_MERGE = 512
TM_EXPERT = 256
TM_COMBINE = 512
LIN_GROUP = 256
MOD_COLS = 1536
NEG_BIG = -1e30

_C_MOBA = (0, 768)
_C_GLA = (768, 1664)
_C_RET = (1664, 2688)
_C_SWA = (2688, 3456)
N_PACKED = 3456


def _dot(a, b, precision=None):
    return jnp.dot(a, b, preferred_element_type=F32, precision=precision)


def _dot_nt(a, b, precision=None):
    return lax.dot_general(a, b, (((1,), (1,)), ((), ())), preferred_element_type=F32, precision=precision)


def _dot_tn(a, b, precision=None):
    return lax.dot_general(a, b, (((0,), (0,)), ((), ())), preferred_element_type=F32, precision=precision)


def _norm_mod(x, g, scale, shift):
    ms = jnp.mean(x * x, axis=-1, keepdims=True)
    y = x * lax.rsqrt(ms + NORM_EPS) * g
    return y * (1.0 + scale) + shift


def _log_sigmoid(x):
    return jnp.minimum(x, 0.0) - jnp.log1p(jnp.exp(-jnp.abs(x)))


def _params(*sem):
    return pltpu.CompilerParams(dimension_semantics=tuple(sem))


def _mod_kernel(c_ref, w_ref, b_ref, o_ref):
    c = c_ref[...]
    o_ref[...] = _dot(jax.nn.silu(c), w_ref[...], HI) + b_ref[...]


def _modulation(c, w_ada, b_ada):
    depth, d, n = w_ada.shape
    b = c.shape[0]
    return pl.pallas_call(
        _mod_kernel,
        out_shape=jax.ShapeDtypeStruct((depth, b, n), F32),
        grid=(depth, n // MOD_COLS),
        in_specs=[
            pl.BlockSpec((b, d), lambda l, j: (0, 0)),
            pl.BlockSpec((None, d, MOD_COLS), lambda l, j: (l, 0, j)),
            pl.BlockSpec((None, 1, MOD_COLS), lambda l, j: (l, 0, j)),
        ],
        out_specs=pl.BlockSpec((None, b, MOD_COLS), lambda l, j: (l, 0, j)),
        compiler_params=_params("arbitrary", "arbitrary"),
        name="adaln_mod",
    )(c, w_ada, b_ada.reshape(depth, 1, n))


def _in_kernel(x_ref, sc_ref, sh_ref, g_ref, w_ref, wg_ref, bg_ref,
               moba_ref, glaf_ref, glav_ref, retf_ref, retv_ref, swa_ref, kmean_ref):
    h = _norm_mod(x_ref[...], g_ref[...], sc_ref[...], sh_ref[...]).astype(BF16)

    zm = _dot(h, w_ref[:, _C_MOBA[0]:_C_MOBA[1]])
    moba_ref[...] = zm.astype(BF16)
    for blk in range(TM_IN // MOBA_BLOCK):
        kb = zm[blk * MOBA_BLOCK:(blk + 1) * MOBA_BLOCK, 256:512]
        kmean_ref[blk:blk + 1, :] = jnp.mean(kb, axis=0, keepdims=True)

    zg = _dot(h, w_ref[:, _C_GLA[0]:_C_GLA[1]])
    gate_logit = _dot(zg[:, 512:640], wg_ref[...], HI) + bg_ref[...]
    glaf_ref[:, 0:128] = zg[:, 0:128] * (GLA_KEY_DIM ** -0.5)
    glaf_ref[:, 128:256] = zg[:, 128:256]
    glaf_ref[:, 256:384] = _log_sigmoid(gate_logit) / GLA_GATE_TEMP
    glaf_ref[:, 384:640] = zg[:, 640:896]
    glav_ref[...] = zg[:, 256:512].astype(BF16)

    zr = _dot(h, w_ref[:, _C_RET[0]:_C_RET[1]])
    retf_ref[:, 0:512] = zr[:, 0:512]
    retf_ref[:, 512:768] = zr[:, 768:1024]
    retv_ref[...] = zr[:, 512:768].astype(BF16)

    swa_ref[...] = _dot(h, w_ref[:, _C_SWA[0]:_C_SWA[1]]).astype(BF16)


def _input_projection(x2, scale, shift, g, w_packed, wg, bg, seq):
    t, d = x2.shape
    tiles_per_seq = seq // TM_IN
    nblk = TM_IN // MOBA_BLOCK
    row = lambda i: (i, 0)
    per_batch = lambda i: (i // tiles_per_seq, 0, 0)
    const2 = lambda i: (0, 0)
    outs = pl.pallas_call(
        _in_kernel,
        out_shape=(
            jax.ShapeDtypeStruct((t, 768), BF16),
            jax.ShapeDtypeStruct((t, 640), F32),
            jax.ShapeDtypeStruct((t, 256), BF16),
            jax.ShapeDtypeStruct((t, 768), F32),
            jax.ShapeDtypeStruct((t, 256), BF16),
            jax.ShapeDtypeStruct((t, 768), BF16),
            jax.ShapeDtypeStruct((t // TM_IN, nblk, 256), F32),
        ),
        grid=(t // TM_IN,),
        in_specs=[
            pl.BlockSpec((TM_IN, d), row),
            pl.BlockSpec((None, 1, d), per_batch),
            pl.BlockSpec((None, 1, d), per_batch),
            pl.BlockSpec((1, d), const2),
            pl.BlockSpec((d, N_PACKED), const2),
            pl.BlockSpec((LANES, LANES), const2),
            pl.BlockSpec((1, LANES), const2),
        ],
        out_specs=(
            pl.BlockSpec((TM_IN, 768), row),
            pl.BlockSpec((TM_IN, 640), row),
            pl.BlockSpec((TM_IN, 256), row),
            pl.BlockSpec((TM_IN, 768), row),
            pl.BlockSpec((TM_IN, 256), row),
            pl.BlockSpec((TM_IN, 768), row),
            pl.BlockSpec((None, nblk, 256), lambda i: (i, 0, 0)),
        ),
        compiler_params=_params("arbitrary"),
        name="input_proj",
    )(x2, scale, shift, g, w_packed, wg, bg)
    return outs


def _moba_kernel(q_ref, k_ref, v_ref, km_ref, o_ref, sel_sc, m_sc, l_sc, acc_sc, *, slopes):
    i = pl.program_id(1)
    nb = MOBA_BLOCK
    n_blocks = km_ref.shape[0]
    q = q_ref[...]
    lane = lax.broadcasted_iota(I32, (1, BRANCH_WIDTH), 1)
    head_of_lane = lane // HEAD_DIM

    rows = lax.broadcasted_iota(I32, (LANES, BRANCH_WIDTH), 0)
    lanes_full = lax.broadcasted_iota(I32, (LANES, BRANCH_WIDTH), 1)
    km = km_ref[...]
    km_t = jnp.concatenate([km] * (LANES // n_blocks), axis=0)
    km_heads = jnp.where((rows // n_blocks) == (lanes_full // HEAD_DIM), km_t, 0.0)
    gate_t = _dot_nt(km_heads, q.astype(F32), HI)

    jrow = lax.broadcasted_iota(I32, (n_blocks, nb), 0)
    valid = jrow < i
    for h in range(BRANCH_HEADS):
        g = jnp.where(valid, gate_t[h * n_blocks:(h + 1) * n_blocks, :], -jnp.inf)
        cnt = jnp.zeros((n_blocks, nb), F32)
        for jp in range(n_blocks):
            gj = g[jp:jp + 1, :]
            beats = (gj > g) | ((gj == g) & (jp < jrow))
            cnt = cnt + beats.astype(F32)
        sel_sc[h * n_blocks:(h + 1) * n_blocks, :] = ((cnt < MOBA_TOPK) & valid).astype(F32)

    key_io = lax.broadcasted_iota(I32, (nb, nb), 0)
    qry_io = lax.broadcasted_iota(I32, (nb, nb), 1)
    rel = (qry_io - key_io).astype(F32)

    m_sc[...] = jnp.full(m_sc.shape, -jnp.inf, F32)
    l_sc[...] = jnp.zeros(l_sc.shape, F32)
    acc_sc[...] = jnp.zeros(acc_sc.shape, F32)

    def process(j, own):
        start = pl.multiple_of(j * nb, nb)
        kj = k_ref[pl.ds(start, nb), :]
        vj_t = v_ref[pl.ds(start, nb), :].astype(F32).T
        dist = rel + ((i - j) * nb).astype(F32)
        for h in range(BRANCH_HEADS):
            hm = head_of_lane == h
            s = _dot_nt(kj, jnp.where(hm, q, jnp.zeros_like(q))) - slopes[h] * dist
            if own:
                mask = rel >= 0.0
            else:
                mask = sel_sc[pl.ds(h * n_blocks + j, 1), :] > 0.5
            s = jnp.where(mask, s, -jnp.inf)
            m_old = m_sc[h]
            m_new = jnp.maximum(m_old, jnp.max(s, axis=0, keepdims=True))
            alpha = jnp.exp(m_old - m_new)
            p = jnp.exp(s - m_new)
            l_sc[h] = alpha * l_sc[h] + jnp.sum(p, axis=0, keepdims=True)
            m_sc[h] = m_new
            hrows = slice(h * HEAD_DIM, (h + 1) * HEAD_DIM)
            pv = _dot(vj_t[hrows, :].astype(BF16), p.astype(BF16))
            acc_sc[hrows, :] = acc_sc[hrows, :] * alpha + pv

    process(i, True)

    def body(j, carry):
        process(j, False)
        return carry

    lax.fori_loop(0, i, body, 0)

    for h in range(BRANCH_HEADS):
        hrows = slice(h * HEAD_DIM, (h + 1) * HEAD_DIM)
        acc_sc[hrows, :] = acc_sc[hrows, :] * (1.0 / l_sc[h])
    o_ref[...] = acc_sc[...].T.astype(BF16)


def _moba(qkv, kmean, batch, seq, slopes):
    t = qkv.shape[0]
    nq = seq // MOBA_BLOCK
    kern = functools.partial(_moba_kernel, slopes=slopes)
    return pl.pallas_call(
        kern,
        out_shape=jax.ShapeDtypeStruct((t, BRANCH_WIDTH), BF16),
        grid=(batch, nq),
        in_specs=[
            pl.BlockSpec((MOBA_BLOCK, BRANCH_WIDTH), lambda b, i: (b * nq + i, 0)),
            pl.BlockSpec((seq, BRANCH_WIDTH), lambda b, i: (b, 1)),
            pl.BlockSpec((seq, BRANCH_WIDTH), lambda b, i: (b, 2)),
            pl.BlockSpec((None, nq, BRANCH_WIDTH), lambda b, i: (b, 0, 0)),
        ],
        out_specs=pl.BlockSpec((MOBA_BLOCK, BRANCH_WIDTH), lambda b, i: (b * nq + i, 0)),
        scratch_shapes=[
            pltpu.VMEM((BRANCH_HEADS * nq, MOBA_BLOCK), F32),
            pltpu.VMEM((BRANCH_HEADS, 1, MOBA_BLOCK), F32),
            pltpu.VMEM((BRANCH_HEADS, 1, MOBA_BLOCK), F32),
            pltpu.VMEM((BRANCH_WIDTH, MOBA_BLOCK), F32),
        ],
        compiler_params=_params("arbitrary", "arbitrary"),
        name="moba_attn",
    )(qkv, qkv, qkv, kmean)


def _swa_kernel(q_ref, kp_ref, kc_ref, vp_ref, vc_ref, sink_ref, o_ref, *, slopes):
    n = pl.program_id(1)
    w = WINDOW
    q = q_ref[...]
    kp, kc, vp, vc = kp_ref[...], kc_ref[...], vp_ref[...], vc_ref[...]
    head_of_lane = lax.broadcasted_iota(I32, (1, BRANCH_WIDTH), 1) // HEAD_DIM
    key_io = lax.broadcasted_iota(I32, (w, w), 0)
    qry_io = lax.broadcasted_iota(I32, (w, w), 1)
    rel = (qry_io - key_io).astype(F32)
    allow_cur = rel >= 0.0
    allow_prev = (rel < 0.0) & (n > 0)
    zq = jnp.zeros_like(q)
    vp_t = vp.astype(F32).T
    vc_t = vc.astype(F32).T
    outs = []
    for h in range(BRANCH_HEADS):
        qh = jnp.where(head_of_lane == h, q, zq)
        sp = jnp.where(allow_prev, _dot_nt(kp, qh) - slopes[h] * (rel + float(w)), -jnp.inf)
        sc = jnp.where(allow_cur, _dot_nt(kc, qh) - slopes[h] * rel, -jnp.inf)
        sink = sink_ref[0:1, h:h + 1]
        m = jnp.maximum(jnp.maximum(jnp.max(sp, axis=0, keepdims=True), jnp.max(sc, axis=0, keepdims=True)), sink)
        pp = jnp.exp(sp - m)
        pc = jnp.exp(sc - m)
        l = jnp.sum(pp, axis=0, keepdims=True) + jnp.sum(pc, axis=0, keepdims=True) + jnp.exp(sink - m)
        hrows = slice(h * HEAD_DIM, (h + 1) * HEAD_DIM)
        o = _dot(vp_t[hrows, :].astype(BF16), pp.astype(BF16)) + _dot(vc_t[hrows, :].astype(BF16), pc.astype(BF16))
        outs.append(o * (1.0 / l))
    o_ref[...] = jnp.concatenate(outs, axis=0).T.astype(BF16)


def _swa(qkv, sinks_row, batch, seq, slopes):
    t = qkv.shape[0]
    nq = seq // WINDOW
    kern = functools.partial(_swa_kernel, slopes=slopes)
    cur = lambda col: (lambda b, n: (b * nq + n, col))
    prev = lambda col: (lambda b, n: (b * nq + jnp.maximum(n - 1, 0), col))
    blk = (WINDOW, BRANCH_WIDTH)
    return pl.pallas_call(
        kern,
        out_shape=jax.ShapeDtypeStruct((t, BRANCH_WIDTH), BF16),
        grid=(batch, nq),
        in_specs=[
            pl.BlockSpec(blk, cur(0)),
            pl.BlockSpec(blk, prev(1)),
            pl.BlockSpec(blk, cur(1)),
            pl.BlockSpec(blk, prev(2)),
            pl.BlockSpec(blk, cur(2)),
            pl.BlockSpec((1, LANES), lambda b, n: (0, 0)),
        ],
        out_specs=pl.BlockSpec(blk, cur(0)),
        compiler_params=_params("arbitrary", "arbitrary"),
        name="swa_attn",
    )(qkv, qkv, qkv, qkv, qkv, sinks_row)


def _lin_kernel(f_ref, v_ref, g_ref, lg_ref, o_ref, st_sc, *, dk, gated, qoff, koff, laoff, roff, norm):
    width = BRANCH_HEADS * dk
    grp = LIN_GROUP
    ch = LINEAR_CHUNK
    seq = f_ref.shape[0]

    r_io = lax.broadcasted_iota(I32, (grp, grp), 0)
    c_io = lax.broadcasted_iota(I32, (grp, grp), 1)
    same = (r_io // ch) == (c_io // ch)
    ltri = (same & (c_io <= r_io)).astype(F32)
    lfull = same.astype(F32)
    head_avg = ((r_io // HEAD_DIM) == (c_io // HEAD_DIM)).astype(F32) * (1.0 / HEAD_DIM)
    lane_k = lax.broadcasted_iota(I32, (1, width), 1) // dk
    lane_v = lax.broadcasted_iota(I32, (1, BRANCH_WIDTH), 1) // HEAD_DIM
    bd_mask = (lax.broadcasted_iota(I32, (BRANCH_WIDTH, width), 0) // HEAD_DIM
               == lax.broadcasted_iota(I32, (BRANCH_WIDTH, width), 1) // dk)
    gain = g_ref[...]

    st_sc[...] = jnp.zeros(st_sc.shape, F32)

    def group(gi, carry):
        r0 = pl.multiple_of(gi * grp, grp)
        rows = pl.ds(r0, grp)
        q = f_ref[rows, qoff:qoff + width]
        k = f_ref[rows, koff:koff + width]
        if gated:
            la = f_ref[rows, laoff:laoff + width]
        else:
            la = jnp.broadcast_to(lg_ref[...], (grp, width))
        b = _dot(ltri, la, HI)
        be = _dot(lfull, la, HI)
        qd = (q * jnp.exp(b)).astype(BF16)
        kin = (k * jnp.exp(-b)).astype(BF16)
        kend = (k * jnp.exp(be - b)).astype(BF16)
        dec = jnp.exp(be)
        v = v_ref[rows, :]
        zq = jnp.zeros_like(qd)
        zv = jnp.zeros_like(v)

        o = jnp.zeros((grp, BRANCH_WIDTH), F32)
        for h in range(BRANCH_HEADS):
            a = _dot_nt(jnp.where(lane_k == h, qd, zq), kin) * ltri
            o = o + _dot(a.astype(BF16), jnp.where(lane_v == h, v, zv))

        st = st_sc[...]
        parts = []
        for c in range(grp // ch):
            sl = slice(c * ch, (c + 1) * ch)
            parts.append(_dot_nt(qd[sl], st.astype(BF16)))
            kv_t = _dot_tn(v[sl], kend[sl])
            st = st * dec[c * ch:c * ch + 1, :] + jnp.where(bd_mask, kv_t, 0.0)
        st_sc[...] = st
        o = o + jnp.concatenate(parts, axis=0)

        if norm == "rms":
            ms = _dot(o * o, head_avg, HI)
            y = o * lax.rsqrt(ms + NORM_EPS) * gain
        else:
            mu = _dot(o, head_avg, HI)
            xc = o - mu
            var = _dot(xc * xc, head_avg, HI)
            y = xc * lax.rsqrt(var + NORM_EPS) * gain
        r = f_ref[rows, roff:roff + BRANCH_WIDTH]
        o_ref[rows, :] = (y * jax.nn.silu(r)).astype(BF16)
        return carry

    lax.fori_loop(0, seq // grp, group, 0)


def _linear_attention(feat, v, gain, log_gamma_row, batch, seq, *, dk, gated, qoff, koff, laoff, roff, norm, name):
    t, fw = feat.shape
    width = BRANCH_HEADS * dk
    kern = functools.partial(_lin_kernel, dk=dk, gated=gated, qoff=qoff, koff=koff, laoff=laoff, roff=roff, norm=norm)
    return pl.pallas_call(
        kern,
        out_shape=jax.ShapeDtypeStruct((t, BRANCH_WIDTH), BF16),
        grid=(batch,),
        in_specs=[
            pl.BlockSpec((seq, fw), lambda b: (b, 0)),
            pl.BlockSpec((seq, BRANCH_WIDTH), lambda b: (b, 0)),
            pl.BlockSpec((1, BRANCH_WIDTH), lambda b: (0, 0)),
            pl.BlockSpec((1, width), lambda b: (0, 0)),
        ],
        out_specs=pl.BlockSpec((seq, BRANCH_WIDTH), lambda b: (b, 0)),
        scratch_shapes=[pltpu.VMEM((BRANCH_WIDTH, width), F32)],
        compiler_params=_params("arbitrary"),
        name=name,
    )(feat, v, gain, log_gamma_row)


def _merge_kernel(x_ref, sc1_ref, sh1_ref, gt1_ref, gnm_ref, ym_ref, yg_ref, yr_ref, ys_ref,
                  wm_ref, wb_ref, wo_ref, gnf_ref, sc2_ref, sh2_ref, wr_ref, br_ref,
                  x1_ref, h2_ref, ti_ref, tw_ref):
    x = x_ref[...]
    d = x.shape[1]
    hb = _norm_mod(x, gnm_ref[...], sc1_ref[...], sh1_ref[...]).astype(BF16)
    mixed = jnp.zeros(x.shape, F32)
    for n, y_ref in enumerate((ym_ref, yg_ref, yr_ref, ys_ref)):
        gate = jax.nn.sigmoid(_dot(hb, wm_ref[:, n * d:(n + 1) * d]))
        mixed = mixed + gate * _dot(y_ref[...], wb_ref[n])
    x1 = x + gt1_ref[...] * _dot(mixed.astype(BF16), wo_ref[...])
    x1_ref[...] = x1

    h2 = _norm_mod(x1, gnf_ref[...], sc2_ref[...], sh2_ref[...])
    sub = d // LANES
    for c in range(sub):
        h2_ref[pl.ds(c, x.shape[0], stride=sub), :] = h2[:, c * LANES:(c + 1) * LANES]
    logits = _dot(h2, wr_ref[...], HI) + br_ref[...]
    lane = lax.broadcasted_iota(I32, logits.shape, 1)
    vals, idxs = [], []
    cur = logits
    for _ in range(TOP_K):
        m = jnp.max(cur, axis=1, keepdims=True)
        idx = jnp.min(jnp.where(cur == m, lane, LANES), axis=1, keepdims=True)
        vals.append(m)
        idxs.append(idx)
        cur = jnp.where(lane == idx, -jnp.inf, cur)
    es = [jnp.exp(v - vals[0]) for v in vals]
    tot = es[0] + es[1] + es[2] + es[3]
    ti = jnp.zeros(logits.shape, I32)
    tw = jnp.zeros(logits.shape, F32)
    for k in range(TOP_K):
        ti = jnp.where(lane == k, idxs[k], ti)
        tw = jnp.where(lane == k, es[k] / tot, tw)
    ti_ref[...] = ti
    tw_ref[...] = tw


def _merge(x2, sc1, sh1, gt1, gnm, ys, wm, wb, wo, gnf, sc2, sh2, wr, br, seq):
    t, d = x2.shape
    tiles_per_seq = seq // TM_MERGE
    row = lambda i: (i, 0)
    per_batch = lambda i: (i // tiles_per_seq, 0, 0)
    c2 = lambda i: (0, 0)
    c3 = lambda i: (0, 0, 0)
    vec = pl.BlockSpec((None, 1, d), per_batch)
    ytile = pl.BlockSpec((TM_MERGE, BRANCH_WIDTH), row)
    return pl.pallas_call(
        _merge_kernel,
        out_shape=(
            jax.ShapeDtypeStruct((t, d), F32),
            jax.ShapeDtypeStruct((t * (d // LANES), LANES), F32),
            jax.ShapeDtypeStruct((t, LANES), I32),
            jax.ShapeDtypeStruct((t, LANES), F32),
        ),
        grid=(t // TM_MERGE,),
        in_specs=[
            pl.BlockSpec((TM_MERGE, d), row), vec, vec, vec,
            pl.BlockSpec((1, d), c2),
            ytile, ytile, ytile, ytile,
            pl.BlockSpec((d, N_BRANCHES * d), c2),
            pl.BlockSpec((N_BRANCHES, BRANCH_WIDTH, d), c3),
            pl.BlockSpec((d, d), c2),
            pl.BlockSpec((1, d), c2), vec, vec,
            pl.BlockSpec((d, LANES), c2),
            pl.BlockSpec((1, LANES), c2),
        ],
        out_specs=(
            pl.BlockSpec((TM_MERGE, d), row),
            pl.BlockSpec((TM_MERGE * (d // LANES), LANES), row),
            pl.BlockSpec((TM_MERGE, LANES), row),
            pl.BlockSpec((TM_MERGE, LANES), row),
        ),
        compiler_params=_params("arbitrary"),
        name="merge_router",
    )(x2, sc1, sh1, gt1, gnm, *ys, wm, wb, wo, gnf, sc2, sh2, wr, br)


def _route(top_i, top_w, n_tok, n_tiles):
    tm = TM_EXPERT
    flat_e = top_i.reshape(-1)
    flat_w = top_w.reshape(-1)
    n_assign = flat_e.shape[0]
    sorted_e, order, sorted_w = lax.sort((flat_e, jnp.arange(n_assign, dtype=I32), flat_w), num_keys=1, is_stable=True)
    counts = jnp.sum((flat_e[:, None] == jnp.arange(N_EXPERTS, dtype=I32)[None, :]).astype(I32), axis=0)
    bounds = jnp.concatenate([jnp.zeros((1,), I32), jnp.cumsum(counts).astype(I32)])
    padded = ((counts + tm - 1) // tm) * tm
    gend = jnp.cumsum(padded)
    gstart = gend - padded
    tile_start = jnp.arange(n_tiles, dtype=I32) * tm
    tile_e_raw = jnp.sum((tile_start[:, None] >= gend[None, :]).astype(I32), axis=1)
    tile_e = jnp.minimum(tile_e_raw, N_EXPERTS - 1).astype(I32)
    n_used = (gend[-1] // tm).astype(I32).reshape(1)
    t_off = tile_start - gstart[tile_e]
    t_cnt = jnp.where(tile_e_raw < N_EXPERTS, counts[tile_e], 0)
    row = jnp.arange(tm, dtype=I32)[None, :]
    j = t_off[:, None] + row
    valid = j < t_cnt[:, None]
    p = jnp.clip(bounds[tile_e][:, None] + j, 0, n_assign - 1)
    a = order[p]
    tok = a // TOP_K
    kk = a % TOP_K
    src = jnp.where(valid, tok, 0).astype(I32)
    dst = jnp.where(valid, kk * n_tok + tok, TOP_K * n_tok + row).astype(I32)
    w = jnp.where(valid, sorted_w[p], 0.0).astype(F32)
    return tile_e, n_used, src.reshape(-1), dst.reshape(-1), w.reshape(-1, 1)


def _moe_kernel(te_ref, nu_ref, src_hbm, dst_hbm, w_ref, h_hbm, wgu_ref, bgu_ref, wd_ref, bd_ref,
                y_hbm, src_s, dst_s, xs, ys, wgu_b, wd_b, isem, gsem, ssem):
    i = pl.program_id(0)
    tm = TM_EXPERT
    sub = xs.shape[1] // tm
    d = sub * LANES
    n_used = nu_ref[0]

    def tok_rows(t):
        return pl.ds(pl.multiple_of(t * sub, sub), sub)

    def idx_copies(tile, buf):
        base = pl.multiple_of(tile * tm, tm)
        return (pltpu.make_async_copy(src_hbm.at[pl.ds(base, tm)], src_s.at[buf], isem.at[buf, 0]),
                pltpu.make_async_copy(dst_hbm.at[pl.ds(base, tm)], dst_s.at[buf], isem.at[buf, 1]))

    def start_idx(tile, buf):
        for cp in idx_copies(tile, buf):
            cp.start()

    def wait_idx(tile, buf):
        for cp in idx_copies(tile, buf):
            cp.wait()

    def issue_gather(buf, slot):
        def body(r, carry):
            pltpu.make_async_copy(h_hbm.at[tok_rows(src_s[buf, r])], xs.at[slot, tok_rows(r)], gsem.at[slot]).start()
            return carry
        lax.fori_loop(0, tm, body, 0, unroll=8)

    def wait_gather(slot):
        pltpu.make_async_copy(h_hbm.at[pl.ds(0, tm * sub)], xs.at[slot], gsem.at[slot]).wait()

    def wait_scatter():
        pltpu.make_async_copy(ys, y_hbm.at[pl.ds(0, tm * sub)], ssem.at[0]).wait()

    @pl.when(i == 0)
    def _():
        ys[...] = jnp.zeros(ys.shape, F32)
        fill = pltpu.make_async_copy(ys, y_hbm.at[pl.ds(y_hbm.shape[0] - tm * sub, tm * sub)], ssem.at[0])
        fill.start()
        fill.wait()

        @pl.when(n_used > 0)
        def _():
            start_idx(0, 0)
            wait_idx(0, 0)
            issue_gather(0, 0)

        @pl.when(n_used > 1)
        def _():
            start_idx(1, 1)

    @pl.when(i < n_used)
    def _():
        slot = lax.rem(i, 2)
        buf = lax.rem(i, 3)

        @pl.when(i + 1 < n_used)
        def _():
            nbuf = lax.rem(i + 1, 3)
            wait_idx(i + 1, nbuf)
            issue_gather(nbuf, 1 - slot)

        @pl.when(i + 2 < n_used)
        def _():
            start_idx(i + 2, lax.rem(i + 2, 3))

        @pl.when((i == 0) | (te_ref[i] != te_ref[jnp.maximum(i - 1, 0)]))
        def _():
            rows = d // 8

            def cast(c, carry):
                r = pl.ds(pl.multiple_of(c * rows, rows), rows)
                wgu_b[r, :] = wgu_ref[r, :].astype(BF16)
                wd_b[r, :] = wd_ref[r, :].astype(BF16)
                return carry
            lax.fori_loop(0, 8, cast, 0)

        wait_gather(slot)
        x = jnp.concatenate([xs[slot, pl.ds(c, tm, stride=sub), :] for c in range(sub)], axis=1).astype(BF16)
        gu = _dot(x, wgu_b[...]) + bgu_ref[...]
        x_glu = jnp.minimum(gu[:, :d], SWIGLU_LIMIT)
        x_lin = jnp.clip(gu[:, d:], -SWIGLU_LIMIT, SWIGLU_LIMIT)
        act = x_glu * jax.nn.sigmoid(SWIGLU_ALPHA * x_glu) * (x_lin + 1.0)
        out = (_dot(act.astype(BF16), wd_b[...]) + bd_ref[...]) * w_ref[...]

        @pl.when(i > 0)
        def _():
            wait_scatter()

        for c in range(sub):
            ys[pl.ds(c, tm, stride=sub), :] = out[:, c * LANES:(c + 1) * LANES]

        def scatter(r, carry):
            pltpu.make_async_copy(ys.at[tok_rows(r)], y_hbm.at[tok_rows(dst_s[buf, r])], ssem.at[0]).start()
            return carry
        lax.fori_loop(0, tm, scatter, 0, unroll=8)

        @pl.when(i == n_used - 1)
        def _():
            wait_scatter()


def _moe(h2_rows, tile_e, n_used, src, dst, w_slot, wgu, bgu, wd, bd, layer):
    d = wgu.shape[2]
    sub = d // LANES
    t = h2_rows.shape[0] // sub
    n_exp = wgu.shape[1]
    n_tiles = tile_e.shape[0]
    tm = TM_EXPERT
    any_spec = pl.BlockSpec(memory_space=pl.ANY)
    expert = lambda i, te, nu: (layer, te[i], 0, 0)
    grid_spec = pltpu.PrefetchScalarGridSpec(
        num_scalar_prefetch=2,
        grid=(n_tiles,),
        in_specs=[
            any_spec, any_spec,
            pl.BlockSpec((tm, 1), lambda i, te, nu: (i, 0)),
            any_spec,
            pl.BlockSpec((None, None, d, 2 * d), expert),
            pl.BlockSpec((None, None, 1, 2 * d), expert),
            pl.BlockSpec((None, None, d, d), expert),
            pl.BlockSpec((None, None, 1, d), expert),
        ],
        out_specs=any_spec,
        scratch_shapes=[
            pltpu.SMEM((3, tm), I32),
            pltpu.SMEM((3, tm), I32),
            pltpu.VMEM((2, tm * sub, LANES), F32),
            pltpu.VMEM((tm * sub, LANES), F32),
            pltpu.VMEM((d, 2 * d), BF16),
            pltpu.VMEM((d, d), BF16),
            pltpu.SemaphoreType.DMA((3, 2)),
            pltpu.SemaphoreType.DMA((2,)),
            pltpu.SemaphoreType.DMA((1,)),
        ],
    )
    return pl.pallas_call(
        _moe_kernel,
        out_shape=jax.ShapeDtypeStruct(((TOP_K * t + tm) * sub, LANES), F32),
        grid_spec=grid_spec,
        compiler_params=pltpu.CompilerParams(dimension_semantics=("arbitrary",), has_side_effects=True),
        name="moe_grouped",
    )(tile_e, n_used, src, dst, w_slot, h2_rows, wgu, bgu.reshape(-1, n_exp, 1, 2 * d), wd, bd.reshape(-1, n_exp, 1, d))


def _combine_kernel(x_ref, gt_ref, y0_ref, y1_ref, y2_ref, y3_ref, gf_ref, o_ref, y_sc, *, final):
    tm, d = x_ref.shape
    sub = d // LANES
    y = (y0_ref[...] + y1_ref[...]) + (y2_ref[...] + y3_ref[...])
    y_sc[...] = y
    y = jnp.concatenate([y_sc[pl.ds(c, tm, stride=sub), :] for c in range(sub)], axis=1)
    x = x_ref[...] + gt_ref[...] * y
    if final:
        ms = jnp.mean(x * x, axis=-1, keepdims=True)
        x = x * lax.rsqrt(ms + NORM_EPS) * gf_ref[...]
    o_ref[...] = x


def _combine(x1, gate2, planes, g_final, seq, final):
    t, d = x1.shape
    tiles_per_seq = seq // TM_COMBINE
    nt = t // TM_COMBINE
    row = lambda i: (i, 0)
    plane = lambda k: (lambda i: (k * nt + i, 0))
    blk = (TM_COMBINE, d)
    pblk = (TM_COMBINE * (d // LANES), LANES)
    return pl.pallas_call(
        functools.partial(_combine_kernel, final=final),
        out_shape=jax.ShapeDtypeStruct((t, d), F32),
        grid=(nt,),
        in_specs=[
            pl.BlockSpec(blk, row),
            pl.BlockSpec((None, 1, d), lambda i: (i // tiles_per_seq, 0, 0)),
            pl.BlockSpec(pblk, plane(0)), pl.BlockSpec(pblk, plane(1)),
            pl.BlockSpec(pblk, plane(2)), pl.BlockSpec(pblk, plane(3)),
            pl.BlockSpec((1, d), lambda i: (0, 0)),
        ],
        out_specs=pl.BlockSpec(blk, row),
        scratch_shapes=[pltpu.VMEM(pblk, F32)],
        compiler_params=_params("arbitrary"),
        name="moe_combine",
    )(x1, gate2, planes, planes, planes, planes, g_final)


def _pack_w_in(w):
    sizes = (256, 256, 256, 128, 128, 256, GLA_GATE_RANK, 256, 256, 256, 256, 256, 256, 128, 128)
    offs = np.concatenate([[0], np.cumsum(sizes)])
    mq, mk, mv, gq, gk, gv, ga, gr, rq, rk, rv, rg, sq, sk, sv = (w[:, int(offs[n]):int(offs[n + 1])] for n in range(15))
    merge = w[:, int(offs[15]):]
    d = w.shape[0]
    rep = lambda kv: jnp.repeat(kv.reshape(d, SWA_KV_HEADS, HEAD_DIM), BRANCH_HEADS // SWA_KV_HEADS, axis=1).reshape(d, BRANCH_WIDTH)
    scale = HEAD_DIM ** -0.5
    ga_pad = jnp.pad(ga, ((0, 0), (0, LANES - GLA_GATE_RANK)))
    packed = jnp.concatenate([mq * scale, mk, mv, gq, gk, gv, ga_pad, gr, rq, rk * scale, rv, rg,
                              sq * scale, rep(sk), rep(sv)], axis=1)
    return packed.astype(BF16), merge.astype(BF16)


def kernel(x, c, w_ada, b_ada, g_norm_mix, w_in, w_gla_gate, b_gla_gate, g_gla_norm, g_ret_norm, attn_sinks,
           w_branch, w_out, g_norm_ffn, w_router, b_router, w_gate_up, b_gate_up, w_down, b_down, g_final):
    batch, seq, d = x.shape
    depth = w_ada.shape[0]
    t = batch * seq
    n_alibi = 2 * BRANCH_HEADS
    slopes = [2.0 ** (-(k + 1.0) * (8.0 / n_alibi)) for k in range(n_alibi)]
    swa_slopes, moba_slopes = tuple(slopes[:BRANCH_HEADS]), tuple(slopes[BRANCH_HEADS:])
    log_gamma = jnp.log(1.0 - 2.0 ** (-RET_DECAY_BASE - jnp.arange(BRANCH_HEADS, dtype=F32)))
    log_gamma_row = jnp.repeat(log_gamma, HEAD_DIM).reshape(1, BRANCH_WIDTH)
    dummy_row = jnp.zeros((1, BRANCH_HEADS * GLA_KEY_DIM), F32)
    n_tiles = (TOP_K * t + N_EXPERTS * (TM_EXPERT - 1) + TM_EXPERT - 1) // TM_EXPERT

    mod = _modulation(c, w_ada, b_ada)
    xf = x.reshape(t, d)
    for l in range(depth):
        shift1, scale1, gate1, shift2, scale2, gate2 = (
            mod[l, :, n * d:(n + 1) * d].reshape(batch, 1, d) for n in range(6))
        w_packed, w_merge = _pack_w_in(w_in[l])
        wg = jnp.pad(w_gla_gate[l], ((0, LANES - GLA_GATE_RANK), (0, 0)))
        moba_qkv, gla_f, gla_v, ret_f, ret_v, swa_qkv, kmean = _input_projection(
            xf, scale1, shift1, g_norm_mix[l].reshape(1, d), w_packed, wg, b_gla_gate[l].reshape(1, -1), seq)
        kmean = kmean.reshape(batch, seq // MOBA_BLOCK, BRANCH_WIDTH)

        y_moba = _moba(moba_qkv, kmean, batch, seq, moba_slopes)
        y_gla = _linear_attention(gla_f, gla_v, g_gla_norm[l].reshape(1, -1), dummy_row, batch, seq,
                                  dk=GLA_KEY_DIM, gated=True, qoff=0, koff=128, laoff=256, roff=384,
                                  norm="rms", name="gla")
        y_ret = _linear_attention(ret_f, ret_v, g_ret_norm[l].reshape(1, -1), log_gamma_row, batch, seq,
                                  dk=HEAD_DIM, gated=False, qoff=0, koff=256, laoff=0, roff=512,
                                  norm="group", name="retention")
        sinks_row = jnp.pad(attn_sinks[l].reshape(1, -1), ((0, 0), (0, LANES - BRANCH_HEADS)))
        y_swa = _swa(swa_qkv, sinks_row, batch, seq, swa_slopes)

        wr = jnp.pad(w_router[l], ((0, 0), (0, LANES - N_EXPERTS)))
        br = jnp.pad(b_router[l].reshape(1, -1), ((0, 0), (0, LANES - N_EXPERTS)), constant_values=NEG_BIG)
        x1, h2, top_i, top_w = _merge(
            xf, scale1, shift1, gate1, g_norm_mix[l].reshape(1, d), (y_moba, y_gla, y_ret, y_swa),
            w_merge, w_branch[l].astype(BF16), w_out[l].astype(BF16),
            g_norm_ffn[l].reshape(1, d), scale2, shift2, wr, br, seq)

        tile_e, n_used, src, dst, w_slot = _route(top_i[:, :TOP_K], top_w[:, :TOP_K], t, n_tiles)
        planes = _moe(h2, tile_e, n_used, src, dst, w_slot, w_gate_up, b_gate_up, w_down, b_down, l)
        xf = _combine(x1, gate2, planes, g_final.reshape(1, d), seq, final=(l == depth - 1))
    return xf.reshape(batch, seq, d)
```

```python
import functools

import jax
import jax.numpy as jnp
import numpy as np
from jax import lax
from jax.experimental import pallas as pl
from jax.experimental.pallas import tpu as pltpu

F32 = jnp.float32
BF16 = jnp.bfloat16
I32 = jnp.int32
HI = lax.Precision.HIGHEST

HEAD_DIM = 64
N_BRANCHES = 4
BRANCH_HEADS = 4
BRANCH_WIDTH = HEAD_DIM * BRANCH_HEADS
MOBA_BLOCK = 256
MOBA_TOPK = 3
GLA_KEY_DIM = 32
GLA_GATE_RANK = 16
GLA_GATE_TEMP = 16.0
LINEAR_CHUNK = 64
RET_DECAY_BASE = 5.0
SWA_KV_HEADS = 2
WINDOW = 128
N_EXPERTS = 32
TOP_K = 4
SWIGLU_ALPHA = 1.702
SWIGLU_LIMIT = 7.0
NORM_EPS = 1e-5

LANES = 128
TM_IN = 512
TM_MERGE = 512
TM_EXPERT = 512
DMA_UNROLL = 8
TM_COMBINE = 512
LIN_GROUP = 256
MOD_COLS = 1536
NEG_BIG = -1e30

_C_MOBA = (0, 768)
_C_GLA = (768, 1664)
_C_RET = (1664, 2688)
_C_SWA = (2688, 3456)
N_PACKED = 3456


def _dot(a, b, precision=None):
    return jnp.dot(a, b, preferred_element_type=F32, precision=precision)


def _dot_nt(a, b, precision=None):
    return lax.dot_general(a, b, (((1,), (1,)), ((), ())), preferred_element_type=F32, precision=precision)


def _dot_tn(a, b, precision=None):
    return lax.dot_general(a, b, (((0,), (0,)), ((), ())), preferred_element_type=F32, precision=precision)


def _norm_mod(x, g, scale, shift):
    ms = jnp.mean(x * x, axis=-1, keepdims=True)
    y = x * lax.rsqrt(ms + NORM_EPS) * g
    return y * (1.0 + scale) + shift


def _log_sigmoid(x):
    return jnp.minimum(x, 0.0) - jnp.log1p(jnp.exp(-jnp.abs(x)))


def _params(*sem):
    return pltpu.CompilerParams(dimension_semantics=tuple(sem))


def _mod_kernel(c_ref, w_ref, b_ref, o_ref):
    c = c_ref[...]
    o_ref[...] = _dot(jax.nn.silu(c), w_ref[...], HI) + b_ref[...]


def _modulation(c, w_ada, b_ada):
    depth, d, n = w_ada.shape
    b = c.shape[0]
    return pl.pallas_call(
        _mod_kernel,
        out_shape=jax.ShapeDtypeStruct((depth, b, n), F32),
        grid=(depth, n // MOD_COLS),
        in_specs=[
            pl.BlockSpec((b, d), lambda l, j: (0, 0)),
            pl.BlockSpec((None, d, MOD_COLS), lambda l, j: (l, 0, j)),
            pl.BlockSpec((None, 1, MOD_COLS), lambda l, j: (l, 0, j)),
        ],
        out_specs=pl.BlockSpec((None, b, MOD_COLS), lambda l, j: (l, 0, j)),
        compiler_params=_params("arbitrary", "arbitrary"),
        name="adaln_mod",
    )(c, w_ada, b_ada.reshape(depth, 1, n))


def _in_kernel(x_ref, sc_ref, sh_ref, g_ref, w_ref, wg_ref, bg_ref,
               moba_ref, glaf_ref, glav_ref, retf_ref, retv_ref, swa_ref, kmean_ref):
    h = _norm_mod(x_ref[...], g_ref[...], sc_ref[...], sh_ref[...]).astype(BF16)

    zm = _dot(h, w_ref[:, _C_MOBA[0]:_C_MOBA[1]])
    moba_ref[...] = zm.astype(BF16)
    for blk in range(TM_IN // MOBA_BLOCK):
        kb = zm[blk * MOBA_BLOCK:(blk + 1) * MOBA_BLOCK, 256:512]
        kmean_ref[blk:blk + 1, :] = jnp.mean(kb, axis=0, keepdims=True)

    zg = _dot(h, w_ref[:, _C_GLA[0]:_C_GLA[1]])
    gate_logit = _dot(zg[:, 512:640], wg_ref[...], HI) + bg_ref[...]
    glaf_ref[:, 0:128] = zg[:, 0:128] * (GLA_KEY_DIM ** -0.5)
    glaf_ref[:, 128:256] = zg[:, 128:256]
    glaf_ref[:, 256:384] = _log_sigmoid(gate_logit) / GLA_GATE_TEMP
    glaf_ref[:, 384:640] = zg[:, 640:896]
    glav_ref[...] = zg[:, 256:512].astype(BF16)

    zr = _dot(h, w_ref[:, _C_RET[0]:_C_RET[1]])
    retf_ref[:, 0:512] = zr[:, 0:512]
    retf_ref[:, 512:768] = zr[:, 768:1024]
    retv_ref[...] = zr[:, 512:768].astype(BF16)

    swa_ref[...] = _dot(h, w_ref[:, _C_SWA[0]:_C_SWA[1]]).astype(BF16)


def _input_projection(x2, scale, shift, g, w_packed, wg, bg, seq):
    t, d = x2.shape
    tiles_per_seq = seq // TM_IN
    nblk = TM_IN // MOBA_BLOCK
    row = lambda i: (i, 0)
    per_batch = lambda i: (i // tiles_per_seq, 0, 0)
    const2 = lambda i: (0, 0)
    outs = pl.pallas_call(
        _in_kernel,
        out_shape=(
            jax.ShapeDtypeStruct((t, 768), BF16),
            jax.ShapeDtypeStruct((t, 640), F32),
            jax.ShapeDtypeStruct((t, 256), BF16),
            jax.ShapeDtypeStruct((t, 768), F32),
            jax.ShapeDtypeStruct((t, 256), BF16),
            jax.ShapeDtypeStruct((t, 768), BF16),
            jax.ShapeDtypeStruct((t // TM_IN, nblk, 256), F32),
        ),
        grid=(t // TM_IN,),
        in_specs=[
            pl.BlockSpec((TM_IN, d), row),
            pl.BlockSpec((None, 1, d), per_batch),
            pl.BlockSpec((None, 1, d), per_batch),
            pl.BlockSpec((1, d), const2),
            pl.BlockSpec((d, N_PACKED), const2),
            pl.BlockSpec((LANES, LANES), const2),
            pl.BlockSpec((1, LANES), const2),
        ],
        out_specs=(
            pl.BlockSpec((TM_IN, 768), row),
            pl.BlockSpec((TM_IN, 640), row),
            pl.BlockSpec((TM_IN, 256), row),
            pl.BlockSpec((TM_IN, 768), row),
            pl.BlockSpec((TM_IN, 256), row),
            pl.BlockSpec((TM_IN, 768), row),
            pl.BlockSpec((None, nblk, 256), lambda i: (i, 0, 0)),
        ),
        compiler_params=_params("arbitrary"),
        name="input_proj",
    )(x2, scale, shift, g, w_packed, wg, bg)
    return outs


def _moba_kernel(q_ref, k_ref, v_ref, km_ref, o_ref, sel_sc, m_sc, l_sc, acc_sc, *, slopes):
    i = pl.program_id(1)
    nb = MOBA_BLOCK
    n_blocks = km_ref.shape[0]
    q = q_ref[...]
    lane = lax.broadcasted_iota(I32, (1, BRANCH_WIDTH), 1)
    head_of_lane = lane // HEAD_DIM

    rows = lax.broadcasted_iota(I32, (LANES, BRANCH_WIDTH), 0)
    lanes_full = lax.broadcasted_iota(I32, (LANES, BRANCH_WIDTH), 1)
    km = km_ref[...]
    km_t = jnp.concatenate([km] * (LANES // n_blocks), axis=0)
    km_heads = jnp.where((rows // n_blocks) == (lanes_full // HEAD_DIM), km_t, 0.0)
    gate_t = _dot_nt(km_heads, q.astype(F32), HI)

    jrow = lax.broadcasted_iota(I32, (n_blocks, nb), 0)
    valid = jrow < i
    for h in range(BRANCH_HEADS):
        g = jnp.where(valid, gate_t[h * n_blocks:(h + 1) * n_blocks, :], -jnp.inf)
        cnt = jnp.zeros((n_blocks, nb), F32)
        for jp in range(n_blocks):
            gj = g[jp:jp + 1, :]
            beats = (gj > g) | ((gj == g) & (jp < jrow))
            cnt = cnt + beats.astype(F32)
        sel_sc[h * n_blocks:(h + 1) * n_blocks, :] = ((cnt < MOBA_TOPK) & valid).astype(F32)

    key_io = lax.broadcasted_iota(I32, (nb, nb), 0)
    qry_io = lax.broadcasted_iota(I32, (nb, nb), 1)
    rel = (qry_io - key_io).astype(F32)

    m_sc[...] = jnp.full(m_sc.shape, -jnp.inf, F32)
    l_sc[...] = jnp.zeros(l_sc.shape, F32)
    acc_sc[...] = jnp.zeros(acc_sc.shape, F32)

    def process(j, own):
        start = pl.multiple_of(j * nb, nb)
        kj = k_ref[pl.ds(start, nb), :]
        vj_t = v_ref[pl.ds(start, nb), :].astype(F32).T
        dist = rel + ((i - j) * nb).astype(F32)
        for h in range(BRANCH_HEADS):
            hm = head_of_lane == h
            s = _dot_nt(kj, jnp.where(hm, q, jnp.zeros_like(q))) - slopes[h] * dist
            if own:
                mask = rel >= 0.0
            else:
                mask = sel_sc[pl.ds(h * n_blocks + j, 1), :] > 0.5
            s = jnp.where(mask, s, -jnp.inf)
            m_old = m_sc[h]
            m_new = jnp.maximum(m_old, jnp.max(s, axis=0, keepdims=True))
            alpha = jnp.exp(m_old - m_new)
            p = jnp.exp(s - m_new)
            l_sc[h] = alpha * l_sc[h] + jnp.sum(p, axis=0, keepdims=True)
            m_sc[h] = m_new
            hrows = slice(h * HEAD_DIM, (h + 1) * HEAD_DIM)
            pv = _dot(vj_t[hrows, :].astype(BF16), p.astype(BF16))
            acc_sc[hrows, :] = acc_sc[hrows, :] * alpha + pv

    process(i, True)

    def body(j, carry):
        process(j, False)
        return carry

    lax.fori_loop(0, i, body, 0)

    for h in range(BRANCH_HEADS):
        hrows = slice(h * HEAD_DIM, (h + 1) * HEAD_DIM)
        acc_sc[hrows, :] = acc_sc[hrows, :] * (1.0 / l_sc[h])
    o_ref[...] = acc_sc[...].T.astype(BF16)


def _moba(qkv, kmean, batch, seq, slopes):
    t = qkv.shape[0]
    nq = seq // MOBA_BLOCK
    kern = functools.partial(_moba_kernel, slopes=slopes)
    return pl.pallas_call(
        kern,
        out_shape=jax.ShapeDtypeStruct((t, BRANCH_WIDTH), BF16),
        grid=(batch, nq),
        in_specs=[
            pl.BlockSpec((MOBA_BLOCK, BRANCH_WIDTH), lambda b, i: (b * nq + i, 0)),
            pl.BlockSpec((seq, BRANCH_WIDTH), lambda b, i: (b, 1)),
            pl.BlockSpec((seq, BRANCH_WIDTH), lambda b, i: (b, 2)),
            pl.BlockSpec((None, nq, BRANCH_WIDTH), lambda b, i: (b, 0, 0)),
        ],
        out_specs=pl.BlockSpec((MOBA_BLOCK, BRANCH_WIDTH), lambda b, i: (b * nq + i, 0)),
        scratch_shapes=[
            pltpu.VMEM((BRANCH_HEADS * nq, MOBA_BLOCK), F32),
            pltpu.VMEM((BRANCH_HEADS, 1, MOBA_BLOCK), F32),
            pltpu.VMEM((BRANCH_HEADS, 1, MOBA_BLOCK), F32),
            pltpu.VMEM((BRANCH_WIDTH, MOBA_BLOCK), F32),
        ],
        compiler_params=_params("arbitrary", "arbitrary"),
        name="moba_attn",
    )(qkv, qkv, qkv, kmean)


def _swa_kernel(q_ref, kp_ref, kc_ref, vp_ref, vc_ref, sink_ref, o_ref, *, slopes):
    n = pl.program_id(1)
    w = WINDOW
    q = q_ref[...]
    kp, kc, vp, vc = kp_ref[...], kc_ref[...], vp_ref[...], vc_ref[...]
    head_of_lane = lax.broadcasted_iota(I32, (1, BRANCH_WIDTH), 1) // HEAD_DIM
    key_io = lax.broadcasted_iota(I32, (w, w), 0)
    qry_io = lax.broadcasted_iota(I32, (w, w), 1)
    rel = (qry_io - key_io).astype(F32)
    allow_cur = rel >= 0.0
    allow_prev = (rel < 0.0) & (n > 0)
    zq = jnp.zeros_like(q)
    vp_t = vp.astype(F32).T
    vc_t = vc.astype(F32).T
    outs = []
    for h in range(BRANCH_HEADS):
        qh = jnp.where(head_of_lane == h, q, zq)
        sp = jnp.where(allow_prev, _dot_nt(kp, qh) - slopes[h] * (rel + float(w)), -jnp.inf)
        sc = jnp.where(allow_cur, _dot_nt(kc, qh) - slopes[h] * rel, -jnp.inf)
        sink = sink_ref[0:1, h:h + 1]
        m = jnp.maximum(jnp.maximum(jnp.max(sp, axis=0, keepdims=True), jnp.max(sc, axis=0, keepdims=True)), sink)
        pp = jnp.exp(sp - m)
        pc = jnp.exp(sc - m)
        l = jnp.sum(pp, axis=0, keepdims=True) + jnp.sum(pc, axis=0, keepdims=True) + jnp.exp(sink - m)
        hrows = slice(h * HEAD_DIM, (h + 1) * HEAD_DIM)
        o = _dot(vp_t[hrows, :].astype(BF16), pp.astype(BF16)) + _dot(vc_t[hrows, :].astype(BF16), pc.astype(BF16))
        outs.append(o * (1.0 / l))
    o_ref[...] = jnp.concatenate(outs, axis=0).T.astype(BF16)


def _swa(qkv, sinks_row, batch, seq, slopes):
    t = qkv.shape[0]
    nq = seq // WINDOW
    kern = functools.partial(_swa_kernel, slopes=slopes)
    cur = lambda col: (lambda b, n: (b * nq + n, col))
    prev = lambda col: (lambda b, n: (b * nq + jnp.maximum(n - 1, 0), col))
    blk = (WINDOW, BRANCH_WIDTH)
    return pl.pallas_call(
        kern,
        out_shape=jax.ShapeDtypeStruct((t, BRANCH_WIDTH), BF16),
        grid=(batch, nq),
        in_specs=[
            pl.BlockSpec(blk, cur(0)),
            pl.BlockSpec(blk, prev(1)),
            pl.BlockSpec(blk, cur(1)),
            pl.BlockSpec(blk, prev(2)),
            pl.BlockSpec(blk, cur(2)),
            pl.BlockSpec((1, LANES), lambda b, n: (0, 0)),
        ],
        out_specs=pl.BlockSpec(blk, cur(0)),
        compiler_params=_params("arbitrary", "arbitrary"),
        name="swa_attn",
    )(qkv, qkv, qkv, qkv, qkv, sinks_row)


def _lin_kernel(f_ref, v_ref, g_ref, lg_ref, o_ref, st_sc, *, dk, gated, qoff, koff, laoff, roff, norm):
    width = BRANCH_HEADS * dk
    grp = LIN_GROUP
    ch = LINEAR_CHUNK
    seq = f_ref.shape[0]

    r_io = lax.broadcasted_iota(I32, (grp, grp), 0)
    c_io = lax.broadcasted_iota(I32, (grp, grp), 1)
    same = (r_io // ch) == (c_io // ch)
    ltri = (same & (c_io <= r_io)).astype(F32)
    lfull = same.astype(F32)
    head_avg = ((r_io // HEAD_DIM) == (c_io // HEAD_DIM)).astype(F32) * (1.0 / HEAD_DIM)
    lane_k = lax.broadcasted_iota(I32, (1, width), 1) // dk
    lane_v = lax.broadcasted_iota(I32, (1, BRANCH_WIDTH), 1) // HEAD_DIM
    bd_mask = (lax.broadcasted_iota(I32, (BRANCH_WIDTH, width), 0) // HEAD_DIM
               == lax.broadcasted_iota(I32, (BRANCH_WIDTH, width), 1) // dk)
    gain = g_ref[...]

    st_sc[...] = jnp.zeros(st_sc.shape, F32)

    def group(gi, carry):
        r0 = pl.multiple_of(gi * grp, grp)
        rows = pl.ds(r0, grp)
        q = f_ref[rows, qoff:qoff + width]
        k = f_ref[rows, koff:koff + width]
        if gated:
            la = f_ref[rows, laoff:laoff + width]
        else:
            la = jnp.broadcast_to(lg_ref[...], (grp, width))
        b = _dot(ltri, la, HI)
        be = _dot(lfull, la, HI)
        qd = (q * jnp.exp(b)).astype(BF16)
        kin = (k * jnp.exp(-b)).astype(BF16)
        kend = (k * jnp.exp(be - b)).astype(BF16)
        dec = jnp.exp(be)
        v = v_ref[rows, :]
        zq = jnp.zeros_like(qd)
        zv = jnp.zeros_like(v)

        o = jnp.zeros((grp, BRANCH_WIDTH), F32)
        for h in range(BRANCH_HEADS):
            a = _dot_nt(jnp.where(lane_k == h, qd, zq), kin) * ltri
            o = o + _dot(a.astype(BF16), jnp.where(lane_v == h, v, zv))

        st = st_sc[...]
        parts = []
        for c in range(grp // ch):
            sl = slice(c * ch, (c + 1) * ch)
            parts.append(_dot_nt(qd[sl], st.astype(BF16)))
            kv_t = _dot_tn(v[sl], kend[sl])
            st = st * dec[c * ch:c * ch + 1, :] + jnp.where(bd_mask, kv_t, 0.0)
        st_sc[...] = st
        o = o + jnp.concatenate(parts, axis=0)

        if norm == "rms":
            ms = _dot(o * o, head_avg, HI)
            y = o * lax.rsqrt(ms + NORM_EPS) * gain
        else:
            mu = _dot(o, head_avg, HI)
            xc = o - mu
            var = _dot(xc * xc, head_avg, HI)
            y = xc * lax.rsqrt(var + NORM_EPS) * gain
        r = f_ref[rows, roff:roff + BRANCH_WIDTH]
        o_ref[rows, :] = (y * jax.nn.silu(r)).astype(BF16)
        return carry

    lax.fori_loop(0, seq // grp, group, 0)


def _linear_attention(feat, v, gain, log_gamma_row, batch, seq, *, dk, gated, qoff, koff, laoff, roff, norm, name):
    t, fw = feat.shape
    width = BRANCH_HEADS * dk
    kern = functools.partial(_lin_kernel, dk=dk, gated=gated, qoff=qoff, koff=koff, laoff=laoff, roff=roff, norm=norm)
    return pl.pallas_call(
        kern,
        out_shape=jax.ShapeDtypeStruct((t, BRANCH_WIDTH), BF16),
        grid=(batch,),
        in_specs=[
            pl.BlockSpec((seq, fw), lambda b: (b, 0)),
            pl.BlockSpec((seq, BRANCH_WIDTH), lambda b: (b, 0)),
            pl.BlockSpec((1, BRANCH_WIDTH), lambda b: (0, 0)),
            pl.BlockSpec((1, width), lambda b: (0, 0)),
        ],
        out_specs=pl.BlockSpec((seq, BRANCH_WIDTH), lambda b: (b, 0)),
        scratch_shapes=[pltpu.VMEM((BRANCH_WIDTH, width), F32)],
        compiler_params=_params("arbitrary"),
        name=name,
    )(feat, v, gain, log_gamma_row)


def _merge_kernel(x_ref, sc1_ref, sh1_ref, gt1_ref, gnm_ref, ym_ref, yg_ref, yr_ref, ys_ref,
                  wm_ref, wb_ref, wo_ref, gnf_ref, sc2_ref, sh2_ref, wr_ref, br_ref,
                  x1_ref, h2_ref, ti_ref, tw_ref):
    x = x_ref[...]
    d = x.shape[1]
    hb = _norm_mod(x, gnm_ref[...], sc1_ref[...], sh1_ref[...]).astype(BF16)
    mixed = jnp.zeros(x.shape, F32)
    for n, y_ref in enumerate((ym_ref, yg_ref, yr_ref, ys_ref)):
        gate = jax.nn.sigmoid(_dot(hb, wm_ref[:, n * d:(n + 1) * d]))
        mixed = mixed + gate * _dot(y_ref[...], wb_ref[n])
    x1 = x + gt1_ref[...] * _dot(mixed.astype(BF16), wo_ref[...])
    x1_ref[...] = x1

    h2 = _norm_mod(x1, gnf_ref[...], sc2_ref[...], sh2_ref[...])
    sub = d // LANES
    for c in range(sub):
        h2_ref[pl.ds(c, x.shape[0], stride=sub), :] = h2[:, c * LANES:(c + 1) * LANES]
    logits = _dot(h2, wr_ref[...], HI) + br_ref[...]
    lane = lax.broadcasted_iota(I32, logits.shape, 1)
    vals, idxs = [], []
    cur = logits
    for _ in range(TOP_K):
        m = jnp.max(cur, axis=1, keepdims=True)
        idx = jnp.min(jnp.where(cur == m, lane, LANES), axis=1, keepdims=True)
        vals.append(m)
        idxs.append(idx)
        cur = jnp.where(lane == idx, -jnp.inf, cur)
    es = [jnp.exp(v - vals[0]) for v in vals]
    tot = es[0] + es[1] + es[2] + es[3]
    ti = jnp.zeros(logits.shape, I32)
    tw = jnp.zeros(logits.shape, F32)
    for k in range(TOP_K):
        ti = jnp.where(lane == k, idxs[k], ti)
        tw = jnp.where(lane == k, es[k] / tot, tw)
    ti_ref[...] = ti
    tw_ref[...] = tw


def _merge(x2, sc1, sh1, gt1, gnm, ys, wm, wb, wo, gnf, sc2, sh2, wr, br, seq):
    t, d = x2.shape
    tiles_per_seq = seq // TM_MERGE
    row = lambda i: (i, 0)
    per_batch = lambda i: (i // tiles_per_seq, 0, 0)
    c2 = lambda i: (0, 0)
    c3 = lambda i: (0, 0, 0)
    vec = pl.BlockSpec((None, 1, d), per_batch)
    ytile = pl.BlockSpec((TM_MERGE, BRANCH_WIDTH), row)
    return pl.pallas_call(
        _merge_kernel,
        out_shape=(
            jax.ShapeDtypeStruct((t, d), F32),
            jax.ShapeDtypeStruct((t * (d // LANES), LANES), F32),
            jax.ShapeDtypeStruct((t, LANES), I32),
            jax.ShapeDtypeStruct((t, LANES), F32),
        ),
        grid=(t // TM_MERGE,),
        in_specs=[
            pl.BlockSpec((TM_MERGE, d), row), vec, vec, vec,
            pl.BlockSpec((1, d), c2),
            ytile, ytile, ytile, ytile,
            pl.BlockSpec((d, N_BRANCHES * d), c2),
            pl.BlockSpec((N_BRANCHES, BRANCH_WIDTH, d), c3),
            pl.BlockSpec((d, d), c2),
            pl.BlockSpec((1, d), c2), vec, vec,
            pl.BlockSpec((d, LANES), c2),
            pl.BlockSpec((1, LANES), c2),
        ],
        out_specs=(
            pl.BlockSpec((TM_MERGE, d), row),
            pl.BlockSpec((TM_MERGE * (d // LANES), LANES), row),
            pl.BlockSpec((TM_MERGE, LANES), row),
            pl.BlockSpec((TM_MERGE, LANES), row),
        ),
        compiler_params=_params("arbitrary"),
        name="merge_router",
    )(x2, sc1, sh1, gt1, gnm, *ys, wm, wb, wo, gnf, sc2, sh2, wr, br)


def _route(top_i, top_w, n_tok, n_tiles):
    tm = TM_EXPERT
    flat_e = top_i.reshape(-1)
    flat_w = top_w.reshape(-1)
    n_assign = flat_e.shape[0]
    sorted_e, order, sorted_w = lax.sort((flat_e, jnp.arange(n_assign, dtype=I32), flat_w), num_keys=1, is_stable=True)
    counts = jnp.sum((flat_e[:, None] == jnp.arange(N_EXPERTS, dtype=I32)[None, :]).astype(I32), axis=0)
    bounds = jnp.concatenate([jnp.zeros((1,), I32), jnp.cumsum(counts).astype(I32)])
    padded = ((counts + tm - 1) // tm) * tm
    gend = jnp.cumsum(padded)
    gstart = gend - padded
    tile_start = jnp.arange(n_tiles, dtype=I32) * tm
    tile_e_raw = jnp.sum((tile_start[:, None] >= gend[None, :]).astype(I32), axis=1)
    tile_e = jnp.minimum(tile_e_raw, N_EXPERTS - 1).astype(I32)
    n_used = (gend[-1] // tm).astype(I32).reshape(1)
    t_off = tile_start - gstart[tile_e]
    t_cnt = jnp.where(tile_e_raw < N_EXPERTS, counts[tile_e], 0)
    row = jnp.arange(tm, dtype=I32)[None, :]
    j = t_off[:, None] + row
    valid = j < t_cnt[:, None]
    p = jnp.clip(bounds[tile_e][:, None] + j, 0, n_assign - 1)
    a = order[p]
    tok = a // TOP_K
    kk = a % TOP_K
    src = jnp.where(valid, tok, 0).astype(I32)
    dst = jnp.where(valid, kk * n_tok + tok, TOP_K * n_tok + row).astype(I32)
    w = jnp.where(valid, sorted_w[p], 0.0).astype(F32)
    return tile_e, n_used, src.reshape(-1), dst.reshape(-1), w.reshape(-1, 1)


def _moe_kernel(te_ref, nu_ref, src_hbm, dst_hbm, w_ref, h_hbm, wgu_ref, bgu_ref, wd_ref, bd_ref,
                y_hbm, src_s, dst_s, xs, ys, wgu_b, wd_b, isem, gsem, ssem):
    i = pl.program_id(0)
    tm = TM_EXPERT
    sub = xs.shape[1] // tm
    d = sub * LANES
    n_used = nu_ref[0]

    def tok_rows(t):
        return pl.ds(pl.multiple_of(t * sub, sub), sub)

    def idx_copies(tile, buf):
        base = pl.multiple_of(tile * tm, tm)
        return (pltpu.make_async_copy(src_hbm.at[pl.ds(base, tm)], src_s.at[buf], isem.at[buf, 0]),
                pltpu.make_async_copy(dst_hbm.at[pl.ds(base, tm)], dst_s.at[buf], isem.at[buf, 1]))

    def start_idx(tile, buf):
        for cp in idx_copies(tile, buf):
            cp.start()

    def wait_idx(tile, buf):
        for cp in idx_copies(tile, buf):
            cp.wait()

    def issue_gather(buf, slot):
        def body(g, carry):
            for u in range(DMA_UNROLL):
                r = g * DMA_UNROLL + u
                pltpu.make_async_copy(h_hbm.at[tok_rows(src_s[buf, r])], xs.at[slot, tok_rows(r)],
                                      gsem.at[slot]).start(priority=u % 2)
            return carry
        lax.fori_loop(0, tm // DMA_UNROLL, body, 0)

    def wait_gather(slot):
        pltpu.make_async_copy(h_hbm.at[pl.ds(0, tm * sub)], xs.at[slot], gsem.at[slot]).wait()

    def wait_scatter():
        pltpu.make_async_copy(ys, y_hbm.at[pl.ds(0, tm * sub)], ssem.at[0]).wait()

    @pl.when(i == 0)
    def _():
        ys[...] = jnp.zeros(ys.shape, F32)
        fill = pltpu.make_async_copy(ys, y_hbm.at[pl.ds(y_hbm.shape[0] - tm * sub, tm * sub)], ssem.at[0])
        fill.start()
        fill.wait()

        @pl.when(n_used > 0)
        def _():
            start_idx(0, 0)
            wait_idx(0, 0)
            issue_gather(0, 0)

        @pl.when(n_used > 1)
        def _():
            start_idx(1, 1)

    @pl.when(i < n_used)
    def _():
        slot = lax.rem(i, 2)
        buf = lax.rem(i, 3)

        @pl.when(i + 1 < n_used)
        def _():
            nbuf = lax.rem(i + 1, 3)
            wait_idx(i + 1, nbuf)
            issue_gather(nbuf, 1 - slot)

        @pl.when(i + 2 < n_used)
        def _():
            start_idx(i + 2, lax.rem(i + 2, 3))

        @pl.when((i == 0) | (te_ref[i] != te_ref[jnp.maximum(i - 1, 0)]))
        def _():
            rows = d // 8

            def cast(c, carry):
                r = pl.ds(pl.multiple_of(c * rows, rows), rows)
                wgu_b[r, :] = wgu_ref[r, :].astype(BF16)
                wd_b[r, :] = wd_ref[r, :].astype(BF16)
                return carry
            lax.fori_loop(0, 8, cast, 0)

        wait_gather(slot)
        x = jnp.concatenate([xs[slot, pl.ds(c, tm, stride=sub), :] for c in range(sub)], axis=1).astype(BF16)
        gu = _dot(x, wgu_b[...]) + bgu_ref[...]
        x_glu = jnp.minimum(gu[:, :d], SWIGLU_LIMIT)
        x_lin = jnp.clip(gu[:, d:], -SWIGLU_LIMIT, SWIGLU_LIMIT)
        act = x_glu * jax.nn.sigmoid(SWIGLU_ALPHA * x_glu) * (x_lin + 1.0)
        out = (_dot(act.astype(BF16), wd_b[...]) + bd_ref[...]) * w_ref[...]

        @pl.when(i > 0)
        def _():
            wait_scatter()

        for c in range(sub):
            ys[pl.ds(c, tm, stride=sub), :] = out[:, c * LANES:(c + 1) * LANES]

        def scatter(g, carry):
            for u in range(DMA_UNROLL):
                r = g * DMA_UNROLL + u
                pltpu.make_async_copy(ys.at[tok_rows(r)], y_hbm.at[tok_rows(dst_s[buf, r])],
                                      ssem.at[0]).start(priority=u % 2)
            return carry
        lax.fori_loop(0, tm // DMA_UNROLL, scatter, 0)

        @pl.when(i == n_used - 1)
        def _():
            wait_scatter()


def _moe(h2_rows, tile_e, n_used, src, dst, w_slot, wgu, bgu, wd, bd, layer):
    d = wgu.shape[2]
    sub = d // LANES
    t = h2_rows.shape[0] // sub
    n_exp = wgu.shape[1]
    n_tiles = tile_e.shape[0]
    tm = TM_EXPERT
    any_spec = pl.BlockSpec(memory_space=pl.ANY)
    expert = lambda i, te, nu: (layer, te[i], 0, 0)
    grid_spec = pltpu.PrefetchScalarGridSpec(
        num_scalar_prefetch=2,
        grid=(n_tiles,),
        in_specs=[
            any_spec, any_spec,
            pl.BlockSpec((tm, 1), lambda i, te, nu: (i, 0)),
            any_spec,
            pl.BlockSpec((None, None, d, 2 * d), expert),
            pl.BlockSpec((None, None, 1, 2 * d), expert),
            pl.BlockSpec((None, None, d, d), expert),
            pl.BlockSpec((None, None, 1, d), expert),
        ],
        out_specs=any_spec,
        scratch_shapes=[
            pltpu.SMEM((3, tm), I32),
            pltpu.SMEM((3, tm), I32),
            pltpu.VMEM((2, tm * sub, LANES), F32),
            pltpu.VMEM((tm * sub, LANES), F32),
            pltpu.VMEM((d, 2 * d), BF16),
            pltpu.VMEM((d, d), BF16),
            pltpu.SemaphoreType.DMA((3, 2)),
            pltpu.SemaphoreType.DMA((2,)),
            pltpu.SemaphoreType.DMA((1,)),
        ],
    )
    return pl.pallas_call(
        _moe_kernel,
        out_shape=jax.ShapeDtypeStruct(((TOP_K * t + tm) * sub, LANES), F32),
        grid_spec=grid_spec,
        compiler_params=pltpu.CompilerParams(dimension_semantics=("arbitrary",), has_side_effects=True),
        name="moe_grouped",
    )(tile_e, n_used, src, dst, w_slot, h2_rows, wgu, bgu.reshape(-1, n_exp, 1, 2 * d), wd, bd.reshape(-1, n_exp, 1, d))


def _combine_kernel(x_ref, gt_ref, y0_ref, y1_ref, y2_ref, y3_ref, gf_ref, o_ref, y_sc, *, final):
    tm, d = x_ref.shape
    sub = d // LANES
    y = (y0_ref[...] + y1_ref[...]) + (y2_ref[...] + y3_ref[...])
    y_sc[...] = y
    y = jnp.concatenate([y_sc[pl.ds(c, tm, stride=sub), :] for c in range(sub)], axis=1)
    x = x_ref[...] + gt_ref[...] * y
    if final:
        ms = jnp.mean(x * x, axis=-1, keepdims=True)
        x = x * lax.rsqrt(ms + NORM_EPS) * gf_ref[...]
    o_ref[...] = x


def _combine(x1, gate2, planes, g_final, seq, final):
    t, d = x1.shape
    tiles_per_seq = seq // TM_COMBINE
    nt = t // TM_COMBINE
    row = lambda i: (i, 0)
    plane = lambda k: (lambda i: (k * nt + i, 0))
    blk = (TM_COMBINE, d)
    pblk = (TM_COMBINE * (d // LANES), LANES)
    return pl.pallas_call(
        functools.partial(_combine_kernel, final=final),
        out_shape=jax.ShapeDtypeStruct((t, d), F32),
        grid=(nt,),
        in_specs=[
            pl.BlockSpec(blk, row),
            pl.BlockSpec((None, 1, d), lambda i: (i // tiles_per_seq, 0, 0)),
            pl.BlockSpec(pblk, plane(0)), pl.BlockSpec(pblk, plane(1)),
            pl.BlockSpec(pblk, plane(2)), pl.BlockSpec(pblk, plane(3)),
            pl.BlockSpec((1, d), lambda i: (0, 0)),
        ],
        out_specs=pl.BlockSpec(blk, row),
        scratch_shapes=[pltpu.VMEM(pblk, F32)],
        compiler_params=_params("arbitrary"),
        name="moe_combine",
    )(x1, gate2, planes, planes, planes, planes, g_final)


def _pack_w_in(w):
    sizes = (256, 256, 256, 128, 128, 256, GLA_GATE_RANK, 256, 256, 256, 256, 256, 256, 128, 128)
    offs = np.concatenate([[0], np.cumsum(sizes)])
    mq, mk, mv, gq, gk, gv, ga, gr, rq, rk, rv, rg, sq, sk, sv = (w[:, int(offs[n]):int(offs[n + 1])] for n in range(15))
    merge = w[:, int(offs[15]):]
    d = w.shape[0]
    rep = lambda kv: jnp.repeat(kv.reshape(d, SWA_KV_HEADS, HEAD_DIM), BRANCH_HEADS // SWA_KV_HEADS, axis=1).reshape(d, BRANCH_WIDTH)
    scale = HEAD_DIM ** -0.5
    ga_pad = jnp.pad(ga, ((0, 0), (0, LANES - GLA_GATE_RANK)))
    packed = jnp.concatenate([mq * scale, mk, mv, gq, gk, gv, ga_pad, gr, rq, rk * scale, rv, rg,
                              sq * scale, rep(sk), rep(sv)], axis=1)
    return packed.astype(BF16), merge.astype(BF16)


def kernel(x, c, w_ada, b_ada, g_norm_mix, w_in, w_gla_gate, b_gla_gate, g_gla_norm, g_ret_norm, attn_sinks,
           w_branch, w_out, g_norm_ffn, w_router, b_router, w_gate_up, b_gate_up, w_down, b_down, g_final):
    batch, seq, d = x.shape
    depth = w_ada.shape[0]
    t = batch * seq
    n_alibi = 2 * BRANCH_HEADS
    slopes = [2.0 ** (-(k + 1.0) * (8.0 / n_alibi)) for k in range(n_alibi)]
    swa_slopes, moba_slopes = tuple(slopes[:BRANCH_HEADS]), tuple(slopes[BRANCH_HEADS:])
    log_gamma = jnp.log(1.0 - 2.0 ** (-RET_DECAY_BASE - jnp.arange(BRANCH_HEADS, dtype=F32)))
    log_gamma_row = jnp.repeat(log_gamma, HEAD_DIM).reshape(1, BRANCH_WIDTH)
    dummy_row = jnp.zeros((1, BRANCH_HEADS * GLA_KEY_DIM), F32)
    n_tiles = (TOP_K * t + N_EXPERTS * (TM_EXPERT - 1) + TM_EXPERT - 1) // TM_EXPERT

    mod = _modulation(c, w_ada, b_ada)
    xf = x.reshape(t, d)
    for l in range(depth):
        shift1, scale1, gate1, shift2, scale2, gate2 = (
            mod[l, :, n * d:(n + 1) * d].reshape(batch, 1, d) for n in range(6))
        w_packed, w_merge = _pack_w_in(w_in[l])
        wg = jnp.pad(w_gla_gate[l], ((0, LANES - GLA_GATE_RANK), (0, 0)))
        moba_qkv, gla_f, gla_v, ret_f, ret_v, swa_qkv, kmean = _input_projection(
            xf, scale1, shift1, g_norm_mix[l].reshape(1, d), w_packed, wg, b_gla_gate[l].reshape(1, -1), seq)
        kmean = kmean.reshape(batch, seq // MOBA_BLOCK, BRANCH_WIDTH)

        y_moba = _moba(moba_qkv, kmean, batch, seq, moba_slopes)
        y_gla = _linear_attention(gla_f, gla_v, g_gla_norm[l].reshape(1, -1), dummy_row, batch, seq,
                                  dk=GLA_KEY_DIM, gated=True, qoff=0, koff=128, laoff=256, roff=384,
                                  norm="rms", name="gla")
        y_ret = _linear_attention(ret_f, ret_v, g_ret_norm[l].reshape(1, -1), log_gamma_row, batch, seq,
                                  dk=HEAD_DIM, gated=False, qoff=0, koff=256, laoff=0, roff=512,
                                  norm="group", name="retention")
        sinks_row = jnp.pad(attn_sinks[l].reshape(1, -1), ((0, 0), (0, LANES - BRANCH_HEADS)))
        y_swa = _swa(swa_qkv, sinks_row, batch, seq, swa_slopes)

        wr = jnp.pad(w_router[l], ((0, 0), (0, LANES - N_EXPERTS)))
        br = jnp.pad(b_router[l].reshape(1, -1), ((0, 0), (0, LANES - N_EXPERTS)), constant_values=NEG_BIG)
        x1, h2, top_i, top_w = _merge(
            xf, scale1, shift1, gate1, g_norm_mix[l].reshape(1, d), (y_moba, y_gla, y_ret, y_swa),
            w_merge, w_branch[l].astype(BF16), w_out[l].astype(BF16),
            g_norm_ffn[l].reshape(1, d), scale2, shift2, wr, br, seq)

        tile_e, n_used, src, dst, w_slot = _route(top_i[:, :TOP_K], top_w[:, :TOP_K], t, n_tiles)
        planes = _moe(h2, tile_e, n_used, src, dst, w_slot, w_gate_up, b_gate_up, w_down, b_down, l)
        xf = _combine(x1, gate2, planes, g_final.reshape(1, d), seq, final=(l == depth - 1))
    return xf.reshape(batch, seq, d)
```

```python
import functools

import jax
import jax.numpy as jnp
import numpy as np
from jax import lax
from jax.experimental import pallas as pl
from jax.experimental.pallas import tpu as pltpu

F32 = jnp.float32
BF16 = jnp.bfloat16
I32 = jnp.int32
HI = lax.Precision.HIGHEST

HEAD_DIM = 64
N_BRANCHES = 4
BRANCH_HEADS = 4
BRANCH_WIDTH = HEAD_DIM * BRANCH_HEADS
MOBA_BLOCK = 256
MOBA_TOPK = 3
GLA_KEY_DIM = 32
GLA_GATE_RANK = 16
GLA_GATE_TEMP = 16.0
LINEAR_CHUNK = 64
RET_DECAY_BASE = 5.0
SWA_KV_HEADS = 2
WINDOW = 128
N_EXPERTS = 32
TOP_K = 4
SWIGLU_ALPHA = 1.702
SWIGLU_LIMIT = 7.0
NORM_EPS = 1e-5

LANES = 128
TM_IN = 512
TM_MERGE = 512
TM_EXPERT = 256
ROW_GROUP = 8
TM_COMBINE = 512
LIN_GROUP = 256
MOD_COLS = 1536
NEG_BIG = -1e30

_C_MOBA = (0, 768)
_C_GLA = (768, 1664)
_C_RET = (1664, 2688)
_C_SWA = (2688, 3456)
N_PACKED = 3456


def _dot(a, b, precision=None):
    return jnp.dot(a, b, preferred_element_type=F32, precision=precision)


def _dot_nt(a, b, precision=None):
    return lax.dot_general(a, b, (((1,), (1,)), ((), ())), preferred_element_type=F32, precision=precision)


def _dot_tn(a, b, precision=None):
    return lax.dot_general(a, b, (((0,), (0,)), ((), ())), preferred_element_type=F32, precision=precision)


def _norm_mod(x, g, scale, shift):
    ms = jnp.mean(x * x, axis=-1, keepdims=True)
    y = x * lax.rsqrt(ms + NORM_EPS) * g
    return y * (1.0 + scale) + shift


def _log_sigmoid(x):
    return jnp.minimum(x, 0.0) - jnp.log1p(jnp.exp(-jnp.abs(x)))


def _params(*sem):
    return pltpu.CompilerParams(dimension_semantics=tuple(sem))


def _mod_kernel(c_ref, w_ref, b_ref, o_ref):
    c = c_ref[...]
    o_ref[...] = _dot(jax.nn.silu(c), w_ref[...], HI) + b_ref[...]


def _modulation(c, w_ada, b_ada):
    depth, d, n = w_ada.shape
    b = c.shape[0]
    return pl.pallas_call(
        _mod_kernel,
        out_shape=jax.ShapeDtypeStruct((depth, b, n), F32),
        grid=(depth, n // MOD_COLS),
        in_specs=[
            pl.BlockSpec((b, d), lambda l, j: (0, 0)),
            pl.BlockSpec((None, d, MOD_COLS), lambda l, j: (l, 0, j)),
            pl.BlockSpec((None, 1, MOD_COLS), lambda l, j: (l, 0, j)),
        ],
        out_specs=pl.BlockSpec((None, b, MOD_COLS), lambda l, j: (l, 0, j)),
        compiler_params=_params("arbitrary", "arbitrary"),
        name="adaln_mod",
    )(c, w_ada, b_ada.reshape(depth, 1, n))


def _in_kernel(x_ref, sc_ref, sh_ref, g_ref, w_ref, wg_ref, bg_ref,
               moba_ref, glaf_ref, glav_ref, retf_ref, retv_ref, swa_ref, kmean_ref):
    h = _norm_mod(x_ref[...], g_ref[...], sc_ref[...], sh_ref[...]).astype(BF16)

    zm = _dot(h, w_ref[:, _C_MOBA[0]:_C_MOBA[1]])
    moba_ref[...] = zm.astype(BF16)
    for blk in range(TM_IN // MOBA_BLOCK):
        kb = zm[blk * MOBA_BLOCK:(blk + 1) * MOBA_BLOCK, 256:512]
        kmean_ref[blk:blk + 1, :] = jnp.mean(kb, axis=0, keepdims=True)

    zg = _dot(h, w_ref[:, _C_GLA[0]:_C_GLA[1]])
    gate_logit = _dot(zg[:, 512:640], wg_ref[...], HI) + bg_ref[...]
    glaf_ref[:, 0:128] = zg[:, 0:128] * (GLA_KEY_DIM ** -0.5)
    glaf_ref[:, 128:256] = zg[:, 128:256]
    glaf_ref[:, 256:384] = _log_sigmoid(gate_logit) / GLA_GATE_TEMP
    glaf_ref[:, 384:640] = zg[:, 640:896]
    glav_ref[...] = zg[:, 256:512].astype(BF16)

    zr = _dot(h, w_ref[:, _C_RET[0]:_C_RET[1]])
    retf_ref[:, 0:512] = zr[:, 0:512]
    retf_ref[:, 512:768] = zr[:, 768:1024]
    retv_ref[...] = zr[:, 512:768].astype(BF16)

    swa_ref[...] = _dot(h, w_ref[:, _C_SWA[0]:_C_SWA[1]]).astype(BF16)


def _input_projection(x2, scale, shift, g, w_packed, wg, bg, seq):
    t, d = x2.shape
    tiles_per_seq = seq // TM_IN
    nblk = TM_IN // MOBA_BLOCK
    row = lambda i: (i, 0)
    per_batch = lambda i: (i // tiles_per_seq, 0, 0)
    const2 = lambda i: (0, 0)
    outs = pl.pallas_call(
        _in_kernel,
        out_shape=(
            jax.ShapeDtypeStruct((t, 768), BF16),
            jax.ShapeDtypeStruct((t, 640), F32),
            jax.ShapeDtypeStruct((t, 256), BF16),
            jax.ShapeDtypeStruct((t, 768), F32),
            jax.ShapeDtypeStruct((t, 256), BF16),
            jax.ShapeDtypeStruct((t, 768), BF16),
            jax.ShapeDtypeStruct((t // TM_IN, nblk, 256), F32),
        ),
        grid=(t // TM_IN,),
        in_specs=[
            pl.BlockSpec((TM_IN, d), row),
            pl.BlockSpec((None, 1, d), per_batch),
            pl.BlockSpec((None, 1, d), per_batch),
            pl.BlockSpec((1, d), const2),
            pl.BlockSpec((d, N_PACKED), const2),
            pl.BlockSpec((LANES, LANES), const2),
            pl.BlockSpec((1, LANES), const2),
        ],
        out_specs=(
            pl.BlockSpec((TM_IN, 768), row),
            pl.BlockSpec((TM_IN, 640), row),
            pl.BlockSpec((TM_IN, 256), row),
            pl.BlockSpec((TM_IN, 768), row),
            pl.BlockSpec((TM_IN, 256), row),
            pl.BlockSpec((TM_IN, 768), row),
            pl.BlockSpec((None, nblk, 256), lambda i: (i, 0, 0)),
        ),
        compiler_params=_params("arbitrary"),
        name="input_proj",
    )(x2, scale, shift, g, w_packed, wg, bg)
    return outs


def _moba_kernel(q_ref, k_ref, v_ref, km_ref, o_ref, sel_sc, m_sc, l_sc, acc_sc, *, slopes):
    i = pl.program_id(1)
    nb = MOBA_BLOCK
    n_blocks = km_ref.shape[0]
    q = q_ref[...]
    lane = lax.broadcasted_iota(I32, (1, BRANCH_WIDTH), 1)
    head_of_lane = lane // HEAD_DIM

    rows = lax.broadcasted_iota(I32, (LANES, BRANCH_WIDTH), 0)
    lanes_full = lax.broadcasted_iota(I32, (LANES, BRANCH_WIDTH), 1)
    km = km_ref[...]
    km_t = jnp.concatenate([km] * (LANES // n_blocks), axis=0)
    km_heads = jnp.where((rows // n_blocks) == (lanes_full // HEAD_DIM), km_t, 0.0)
    gate_t = _dot_nt(km_heads, q.astype(F32), HI)

    jrow = lax.broadcasted_iota(I32, (n_blocks, nb), 0)
    valid = jrow < i
    for h in range(BRANCH_HEADS):
        g = jnp.where(valid, gate_t[h * n_blocks:(h + 1) * n_blocks, :], -jnp.inf)
        cnt = jnp.zeros((n_blocks, nb), F32)
        for jp in range(n_blocks):
            gj = g[jp:jp + 1, :]
            beats = (gj > g) | ((gj == g) & (jp < jrow))
            cnt = cnt + beats.astype(F32)
        sel_sc[h * n_blocks:(h + 1) * n_blocks, :] = ((cnt < MOBA_TOPK) & valid).astype(F32)

    key_io = lax.broadcasted_iota(I32, (nb, nb), 0)
    qry_io = lax.broadcasted_iota(I32, (nb, nb), 1)
    rel = (qry_io - key_io).astype(F32)

    m_sc[...] = jnp.full(m_sc.shape, -jnp.inf, F32)
    l_sc[...] = jnp.zeros(l_sc.shape, F32)
    acc_sc[...] = jnp.zeros(acc_sc.shape, F32)

    def process(j, own):
        start = pl.multiple_of(j * nb, nb)
        kj = k_ref[pl.ds(start, nb), :]
        vj_t = v_ref[pl.ds(start, nb), :].astype(F32).T
        dist = rel + ((i - j) * nb).astype(F32)
        for h in range(BRANCH_HEADS):
            hm = head_of_lane == h
            s = _dot_nt(kj, jnp.where(hm, q, jnp.zeros_like(q))) - slopes[h] * dist
            if own:
                mask = rel >= 0.0
            else:
                mask = sel_sc[pl.ds(h * n_blocks + j, 1), :] > 0.5
            s = jnp.where(mask, s, -jnp.inf)
            m_old = m_sc[h]
            m_new = jnp.maximum(m_old, jnp.max(s, axis=0, keepdims=True))
            alpha = jnp.exp(m_old - m_new)
            p = jnp.exp(s - m_new)
            l_sc[h] = alpha * l_sc[h] + jnp.sum(p, axis=0, keepdims=True)
            m_sc[h] = m_new
            hrows = slice(h * HEAD_DIM, (h + 1) * HEAD_DIM)
            pv = _dot(vj_t[hrows, :].astype(BF16), p.astype(BF16))
            acc_sc[hrows, :] = acc_sc[hrows, :] * alpha + pv

    process(i, True)

    def body(j, carry):
        process(j, False)
        return carry

    lax.fori_loop(0, i, body, 0)

    for h in range(BRANCH_HEADS):
        hrows = slice(h * HEAD_DIM, (h + 1) * HEAD_DIM)
        acc_sc[hrows, :] = acc_sc[hrows, :] * (1.0 / l_sc[h])
    o_ref[...] = acc_sc[...].T.astype(BF16)


def _moba(qkv, kmean, batch, seq, slopes):
    t = qkv.shape[0]
    nq = seq // MOBA_BLOCK
    kern = functools.partial(_moba_kernel, slopes=slopes)
    return pl.pallas_call(
        kern,
        out_shape=jax.ShapeDtypeStruct((t, BRANCH_WIDTH), BF16),
        grid=(batch, nq),
        in_specs=[
            pl.BlockSpec((MOBA_BLOCK, BRANCH_WIDTH), lambda b, i: (b * nq + i, 0)),
            pl.BlockSpec((seq, BRANCH_WIDTH), lambda b, i: (b, 1)),
            pl.BlockSpec((seq, BRANCH_WIDTH), lambda b, i: (b, 2)),
            pl.BlockSpec((None, nq, BRANCH_WIDTH), lambda b, i: (b, 0, 0)),
        ],
        out_specs=pl.BlockSpec((MOBA_BLOCK, BRANCH_WIDTH), lambda b, i: (b * nq + i, 0)),
        scratch_shapes=[
            pltpu.VMEM((BRANCH_HEADS * nq, MOBA_BLOCK), F32),
            pltpu.VMEM((BRANCH_HEADS, 1, MOBA_BLOCK), F32),
            pltpu.VMEM((BRANCH_HEADS, 1, MOBA_BLOCK), F32),
            pltpu.VMEM((BRANCH_WIDTH, MOBA_BLOCK), F32),
        ],
        compiler_params=_params("arbitrary", "arbitrary"),
        name="moba_attn",
    )(qkv, qkv, qkv, kmean)


def _swa_kernel(q_ref, kp_ref, kc_ref, vp_ref, vc_ref, sink_ref, o_ref, *, slopes):
    n = pl.program_id(1)
    w = WINDOW
    q = q_ref[...]
    kp, kc, vp, vc = kp_ref[...], kc_ref[...], vp_ref[...], vc_ref[...]
    head_of_lane = lax.broadcasted_iota(I32, (1, BRANCH_WIDTH), 1) // HEAD_DIM
    key_io = lax.broadcasted_iota(I32, (w, w), 0)
    qry_io = lax.broadcasted_iota(I32, (w, w), 1)
    rel = (qry_io - key_io).astype(F32)
    allow_cur = rel >= 0.0
    allow_prev = (rel < 0.0) & (n > 0)
    zq = jnp.zeros_like(q)
    vp_t = vp.astype(F32).T
    vc_t = vc.astype(F32).T
    outs = []
    for h in range(BRANCH_HEADS):
        qh = jnp.where(head_of_lane == h, q, zq)
        sp = jnp.where(allow_prev, _dot_nt(kp, qh) - slopes[h] * (rel + float(w)), -jnp.inf)
        sc = jnp.where(allow_cur, _dot_nt(kc, qh) - slopes[h] * rel, -jnp.inf)
        sink = sink_ref[0:1, h:h + 1]
        m = jnp.maximum(jnp.maximum(jnp.max(sp, axis=0, keepdims=True), jnp.max(sc, axis=0, keepdims=True)), sink)
        pp = jnp.exp(sp - m)
        pc = jnp.exp(sc - m)
        l = jnp.sum(pp, axis=0, keepdims=True) + jnp.sum(pc, axis=0, keepdims=True) + jnp.exp(sink - m)
        hrows = slice(h * HEAD_DIM, (h + 1) * HEAD_DIM)
        o = _dot(vp_t[hrows, :].astype(BF16), pp.astype(BF16)) + _dot(vc_t[hrows, :].astype(BF16), pc.astype(BF16))
        outs.append(o * (1.0 / l))
    o_ref[...] = jnp.concatenate(outs, axis=0).T.astype(BF16)


def _swa(qkv, sinks_row, batch, seq, slopes):
    t = qkv.shape[0]
    nq = seq // WINDOW
    kern = functools.partial(_swa_kernel, slopes=slopes)
    cur = lambda col: (lambda b, n: (b * nq + n, col))
    prev = lambda col: (lambda b, n: (b * nq + jnp.maximum(n - 1, 0), col))
    blk = (WINDOW, BRANCH_WIDTH)
    return pl.pallas_call(
        kern,
        out_shape=jax.ShapeDtypeStruct((t, BRANCH_WIDTH), BF16),
        grid=(batch, nq),
        in_specs=[
            pl.BlockSpec(blk, cur(0)),
            pl.BlockSpec(blk, prev(1)),
            pl.BlockSpec(blk, cur(1)),
            pl.BlockSpec(blk, prev(2)),
            pl.BlockSpec(blk, cur(2)),
            pl.BlockSpec((1, LANES), lambda b, n: (0, 0)),
        ],
        out_specs=pl.BlockSpec(blk, cur(0)),
        compiler_params=_params("arbitrary", "arbitrary"),
        name="swa_attn",
    )(qkv, qkv, qkv, qkv, qkv, sinks_row)


def _lin_kernel(f_ref, v_ref, g_ref, lg_ref, o_ref, st_sc, *, dk, gated, qoff, koff, laoff, roff, norm):
    width = BRANCH_HEADS * dk
    grp = LIN_GROUP
    ch = LINEAR_CHUNK
    seq = f_ref.shape[0]

    r_io = lax.broadcasted_iota(I32, (grp, grp), 0)
    c_io = lax.broadcasted_iota(I32, (grp, grp), 1)
    same = (r_io // ch) == (c_io // ch)
    ltri = (same & (c_io <= r_io)).astype(F32)
    lfull = same.astype(F32)
    head_avg = ((r_io // HEAD_DIM) == (c_io // HEAD_DIM)).astype(F32) * (1.0 / HEAD_DIM)
    lane_k = lax.broadcasted_iota(I32, (1, width), 1) // dk
    lane_v = lax.broadcasted_iota(I32, (1, BRANCH_WIDTH), 1) // HEAD_DIM
    bd_mask = (lax.broadcasted_iota(I32, (BRANCH_WIDTH, width), 0) // HEAD_DIM
               == lax.broadcasted_iota(I32, (BRANCH_WIDTH, width), 1) // dk)
    gain = g_ref[...]

    st_sc[...] = jnp.zeros(st_sc.shape, F32)

    def group(gi, carry):
        r0 = pl.multiple_of(gi * grp, grp)
        rows = pl.ds(r0, grp)
        q = f_ref[rows, qoff:qoff + width]
        k = f_ref[rows, koff:koff + width]
        if gated:
            la = f_ref[rows, laoff:laoff + width]
        else:
            la = jnp.broadcast_to(lg_ref[...], (grp, width))
        b = _dot(ltri, la, HI)
        be = _dot(lfull, la, HI)
        qd = (q * jnp.exp(b)).astype(BF16)
        kin = (k * jnp.exp(-b)).astype(BF16)
        kend = (k * jnp.exp(be - b)).astype(BF16)
        dec = jnp.exp(be)
        v = v_ref[rows, :]
        zq = jnp.zeros_like(qd)
        zv = jnp.zeros_like(v)

        o = jnp.zeros((grp, BRANCH_WIDTH), F32)
        for h in range(BRANCH_HEADS):
            a = _dot_nt(jnp.where(lane_k == h, qd, zq), kin) * ltri
            o = o + _dot(a.astype(BF16), jnp.where(lane_v == h, v, zv))

        st = st_sc[...]
        parts = []
        for c in range(grp // ch):
            sl = slice(c * ch, (c + 1) * ch)
            parts.append(_dot_nt(qd[sl], st.astype(BF16)))
            kv_t = _dot_tn(v[sl], kend[sl])
            st = st * dec[c * ch:c * ch + 1, :] + jnp.where(bd_mask, kv_t, 0.0)
        st_sc[...] = st
        o = o + jnp.concatenate(parts, axis=0)

        if norm == "rms":
            ms = _dot(o * o, head_avg, HI)
            y = o * lax.rsqrt(ms + NORM_EPS) * gain
        else:
            mu = _dot(o, head_avg, HI)
            xc = o - mu
            var = _dot(xc * xc, head_avg, HI)
            y = xc * lax.rsqrt(var + NORM_EPS) * gain
        r = f_ref[rows, roff:roff + BRANCH_WIDTH]
        o_ref[rows, :] = (y * jax.nn.silu(r)).astype(BF16)
        return carry

    lax.fori_loop(0, seq // grp, group, 0)


def _linear_attention(feat, v, gain, log_gamma_row, batch, seq, *, dk, gated, qoff, koff, laoff, roff, norm, name):
    t, fw = feat.shape
    width = BRANCH_HEADS * dk
    kern = functools.partial(_lin_kernel, dk=dk, gated=gated, qoff=qoff, koff=koff, laoff=laoff, roff=roff, norm=norm)
    return pl.pallas_call(
        kern,
        out_shape=jax.ShapeDtypeStruct((t, BRANCH_WIDTH), BF16),
        grid=(batch,),
        in_specs=[
            pl.BlockSpec((seq, fw), lambda b: (b, 0)),
            pl.BlockSpec((seq, BRANCH_WIDTH), lambda b: (b, 0)),
            pl.BlockSpec((1, BRANCH_WIDTH), lambda b: (0, 0)),
            pl.BlockSpec((1, width), lambda b: (0, 0)),
        ],
        out_specs=pl.BlockSpec((seq, BRANCH_WIDTH), lambda b: (b, 0)),
        scratch_shapes=[pltpu.VMEM((BRANCH_WIDTH, width), F32)],
        compiler_params=_params("arbitrary"),
        name=name,
    )(feat, v, gain, log_gamma_row)


def _merge_kernel(x_ref, sc1_ref, sh1_ref, gt1_ref, gnm_ref, ym_ref, yg_ref, yr_ref, ys_ref,
                  wm_ref, wb_ref, wo_ref, gnf_ref, sc2_ref, sh2_ref, wr_ref, br_ref,
                  x1_ref, h2_ref, ti_ref, tw_ref):
    x = x_ref[...]
    d = x.shape[1]
    hb = _norm_mod(x, gnm_ref[...], sc1_ref[...], sh1_ref[...]).astype(BF16)
    mixed = jnp.zeros(x.shape, F32)
    for n, y_ref in enumerate((ym_ref, yg_ref, yr_ref, ys_ref)):
        gate = jax.nn.sigmoid(_dot(hb, wm_ref[:, n * d:(n + 1) * d]))
        mixed = mixed + gate * _dot(y_ref[...], wb_ref[n])
    x1 = x + gt1_ref[...] * _dot(mixed.astype(BF16), wo_ref[...])
    x1_ref[...] = x1

    h2 = _norm_mod(x1, gnf_ref[...], sc2_ref[...], sh2_ref[...])
    sub = d // LANES
    for c in range(sub):
        h2_ref[pl.ds(c, x.shape[0], stride=sub), :] = h2[:, c * LANES:(c + 1) * LANES]
    logits = _dot(h2, wr_ref[...], HI) + br_ref[...]
    lane = lax.broadcasted_iota(I32, logits.shape, 1)
    vals, idxs = [], []
    cur = logits
    for _ in range(TOP_K):
        m = jnp.max(cur, axis=1, keepdims=True)
        idx = jnp.min(jnp.where(cur == m, lane, LANES), axis=1, keepdims=True)
        vals.append(m)
        idxs.append(idx)
        cur = jnp.where(lane == idx, -jnp.inf, cur)
    es = [jnp.exp(v - vals[0]) for v in vals]
    tot = es[0] + es[1] + es[2] + es[3]
    ti = jnp.zeros(logits.shape, I32)
    tw = jnp.zeros(logits.shape, F32)
    for k in range(TOP_K):
        ti = jnp.where(lane == k, idxs[k], ti)
        tw = jnp.where(lane == k, es[k] / tot, tw)
    ti_ref[...] = ti
    tw_ref[...] = tw


def _merge(x2, sc1, sh1, gt1, gnm, ys, wm, wb, wo, gnf, sc2, sh2, wr, br, seq):
    t, d = x2.shape
    tiles_per_seq = seq // TM_MERGE
    row = lambda i: (i, 0)
    per_batch = lambda i: (i // tiles_per_seq, 0, 0)
    c2 = lambda i: (0, 0)
    c3 = lambda i: (0, 0, 0)
    vec = pl.BlockSpec((None, 1, d), per_batch)
    ytile = pl.BlockSpec((TM_MERGE, BRANCH_WIDTH), row)
    return pl.pallas_call(
        _merge_kernel,
        out_shape=(
            jax.ShapeDtypeStruct((t, d), F32),
            jax.ShapeDtypeStruct((t * (d // LANES), LANES), F32),
            jax.ShapeDtypeStruct((t, LANES), I32),
            jax.ShapeDtypeStruct((t, LANES), F32),
        ),
        grid=(t // TM_MERGE,),
        in_specs=[
            pl.BlockSpec((TM_MERGE, d), row), vec, vec, vec,
            pl.BlockSpec((1, d), c2),
            ytile, ytile, ytile, ytile,
            pl.BlockSpec((d, N_BRANCHES * d), c2),
            pl.BlockSpec((N_BRANCHES, BRANCH_WIDTH, d), c3),
            pl.BlockSpec((d, d), c2),
            pl.BlockSpec((1, d), c2), vec, vec,
            pl.BlockSpec((d, LANES), c2),
            pl.BlockSpec((1, LANES), c2),
        ],
        out_specs=(
            pl.BlockSpec((TM_MERGE, d), row),
            pl.BlockSpec((TM_MERGE * (d // LANES), LANES), row),
            pl.BlockSpec((TM_MERGE, LANES), row),
            pl.BlockSpec((TM_MERGE, LANES), row),
        ),
        compiler_params=_params("arbitrary"),
        name="merge_router",
    )(x2, sc1, sh1, gt1, gnm, *ys, wm, wb, wo, gnf, sc2, sh2, wr, br)


def _route(top_i, top_w, n_tok, n_tiles):
    tm = TM_EXPERT
    flat_e = top_i.reshape(-1)
    flat_w = top_w.reshape(-1)
    n_assign = flat_e.shape[0]
    sorted_e, order, sorted_w = lax.sort((flat_e, jnp.arange(n_assign, dtype=I32), flat_w), num_keys=1, is_stable=True)
    counts = jnp.sum((flat_e[:, None] == jnp.arange(N_EXPERTS, dtype=I32)[None, :]).astype(I32), axis=0)
    bounds = jnp.concatenate([jnp.zeros((1,), I32), jnp.cumsum(counts).astype(I32)])
    padded = ((counts + tm - 1) // tm) * tm
    gend = jnp.cumsum(padded)
    gstart = gend - padded
    tile_start = jnp.arange(n_tiles, dtype=I32) * tm
    tile_e_raw = jnp.sum((tile_start[:, None] >= gend[None, :]).astype(I32), axis=1)
    tile_e = jnp.minimum(tile_e_raw, N_EXPERTS - 1).astype(I32)
    n_used = (gend[-1] // tm).astype(I32).reshape(1)
    t_off = tile_start - gstart[tile_e]
    t_cnt = jnp.where(tile_e_raw < N_EXPERTS, counts[tile_e], 0)
    row = jnp.arange(tm, dtype=I32)[None, :]
    j = t_off[:, None] + row
    valid = j < t_cnt[:, None]
    p = jnp.clip(bounds[tile_e][:, None] + j, 0, n_assign - 1)
    a = order[p]
    tok = a // TOP_K
    kk = a % TOP_K
    src = jnp.where(valid, tok, 0).astype(I32)
    dst = jnp.where(valid, kk * n_tok + tok, TOP_K * n_tok + row).astype(I32)
    w = jnp.where(valid, sorted_w[p], 0.0).astype(F32)
    tile_nv = jnp.clip(t_cnt - t_off, 0, tm).astype(I32)
    return tile_e, n_used, tile_nv, src.reshape(-1), dst.reshape(-1), w.reshape(-1, 1)


def _moe_kernel(te_ref, nu_ref, nv_ref, src_hbm, dst_hbm, w_ref, h_hbm, wgu_ref, bgu_ref, wd_ref, bd_ref,
                y_hbm, src_s, dst_s, xs, ys, wgu_b, wd_b, isem, gsem, ssem):
    i = pl.program_id(0)
    tm = TM_EXPERT
    sub = xs.shape[1] // tm
    d = sub * LANES
    n_used = nu_ref[0]

    def tok_rows(t):
        return pl.ds(pl.multiple_of(t * sub, sub), sub)

    def idx_copies(tile, buf):
        base = pl.multiple_of(tile * tm, tm)
        return (pltpu.make_async_copy(src_hbm.at[pl.ds(base, tm)], src_s.at[buf], isem.at[buf, 0]),
                pltpu.make_async_copy(dst_hbm.at[pl.ds(base, tm)], dst_s.at[buf], isem.at[buf, 1]))

    def start_idx(tile, buf):
        for cp in idx_copies(tile, buf):
            cp.start()

    def wait_idx(tile, buf):
        for cp in idx_copies(tile, buf):
            cp.wait()

    def groups(tile):
        return lax.shift_right_logical(nv_ref[tile] + (ROW_GROUP - 1), ROW_GROUP.bit_length() - 1)

    def moved_rows(tile):
        return pl.multiple_of(groups(tile) * (ROW_GROUP * sub), ROW_GROUP * sub)

    def issue_gather(tile, buf, slot):
        def body(g, carry):
            for u in range(ROW_GROUP):
                r = g * ROW_GROUP + u
                pltpu.make_async_copy(h_hbm.at[tok_rows(src_s[buf, r])], xs.at[slot, tok_rows(r)], gsem.at[slot]).start()
            return carry
        lax.fori_loop(0, groups(tile), body, 0)

    def wait_gather(tile, slot):
        n = moved_rows(tile)
        pltpu.make_async_copy(h_hbm.at[pl.ds(0, n)], xs.at[slot, pl.ds(0, n)], gsem.at[slot]).wait()

    def wait_scatter(tile):
        n = moved_rows(tile)
        pltpu.make_async_copy(ys.at[pl.ds(0, n)], y_hbm.at[pl.ds(0, n)], ssem.at[0]).wait()

    @pl.when(i == 0)
    def _():
        ys[...] = jnp.zeros(ys.shape, F32)
        fill = pltpu.make_async_copy(ys, y_hbm.at[pl.ds(y_hbm.shape[0] - tm * sub, tm * sub)], ssem.at[0])
        fill.start()
        fill.wait()
        xs[...] = jnp.zeros(xs.shape, F32)

        @pl.when(n_used > 0)
        def _():
            start_idx(0, 0)
            wait_idx(0, 0)
            issue_gather(0, 0, 0)

        @pl.when(n_used > 1)
        def _():
            start_idx(1, 1)

    @pl.when(i < n_used)
    def _():
        slot = lax.rem(i, 2)
        buf = lax.rem(i, 3)

        @pl.when(i + 1 < n_used)
        def _():
            nbuf = lax.rem(i + 1, 3)
            wait_idx(i + 1, nbuf)
            issue_gather(i + 1, nbuf, 1 - slot)

        @pl.when(i + 2 < n_used)
        def _():
            start_idx(i + 2, lax.rem(i + 2, 3))

        @pl.when((i == 0) | (te_ref[i] != te_ref[jnp.maximum(i - 1, 0)]))
        def _():
            rows = d // 8

            def cast(c, carry):
                r = pl.ds(pl.multiple_of(c * rows, rows), rows)
                wgu_b[r, :] = wgu_ref[r, :].astype(BF16)
                wd_b[r, :] = wd_ref[r, :].astype(BF16)
                return carry
            lax.fori_loop(0, 8, cast, 0)

        wait_gather(i, slot)
        x = jnp.concatenate([xs[slot, pl.ds(c, tm, stride=sub), :] for c in range(sub)], axis=1).astype(BF16)
        gu = _dot(x, wgu_b[...]) + bgu_ref[...]
        x_glu = jnp.minimum(gu[:, :d], SWIGLU_LIMIT)
        x_lin = jnp.clip(gu[:, d:], -SWIGLU_LIMIT, SWIGLU_LIMIT)
        act = x_glu * jax.nn.sigmoid(SWIGLU_ALPHA * x_glu) * (x_lin + 1.0)
        out = (_dot(act.astype(BF16), wd_b[...]) + bd_ref[...]) * w_ref[...]

        @pl.when(i > 0)
        def _():
            wait_scatter(i - 1)

        for c in range(sub):
            ys[pl.ds(c, tm, stride=sub), :] = out[:, c * LANES:(c + 1) * LANES]

        def scatter(g, carry):
            for u in range(ROW_GROUP):
                r = g * ROW_GROUP + u
                pltpu.make_async_copy(ys.at[tok_rows(r)], y_hbm.at[tok_rows(dst_s[buf, r])], ssem.at[0]).start()
            return carry
        lax.fori_loop(0, groups(i), scatter, 0)

        @pl.when(i == n_used - 1)
        def _():
            wait_scatter(i)


def _moe(h2_rows, tile_e, n_used, tile_nv, src, dst, w_slot, wgu, bgu, wd, bd, layer):
    d = wgu.shape[2]
    sub = d // LANES
    t = h2_rows.shape[0] // sub
    n_exp = wgu.shape[1]
    n_tiles = tile_e.shape[0]
    tm = TM_EXPERT
    any_spec = pl.BlockSpec(memory_space=pl.ANY)
    expert = lambda i, te, nu, nv: (layer, te[i], 0, 0)
    grid_spec = pltpu.PrefetchScalarGridSpec(
        num_scalar_prefetch=3,
        grid=(n_tiles,),
        in_specs=[
            any_spec, any_spec,
            pl.BlockSpec((tm, 1), lambda i, te, nu, nv: (i, 0)),
            any_spec,
            pl.BlockSpec((None, None, d, 2 * d), expert),
            pl.BlockSpec((None, None, 1, 2 * d), expert),
            pl.BlockSpec((None, None, d, d), expert),
            pl.BlockSpec((None, None, 1, d), expert),
        ],
        out_specs=any_spec,
        scratch_shapes=[
            pltpu.SMEM((3, tm), I32),
            pltpu.SMEM((3, tm), I32),
            pltpu.VMEM((2, tm * sub, LANES), F32),
            pltpu.VMEM((tm * sub, LANES), F32),
            pltpu.VMEM((d, 2 * d), BF16),
            pltpu.VMEM((d, d), BF16),
            pltpu.SemaphoreType.DMA((3, 2)),
            pltpu.SemaphoreType.DMA((2,)),
            pltpu.SemaphoreType.DMA((1,)),
        ],
    )
    return pl.pallas_call(
        _moe_kernel,
        out_shape=jax.ShapeDtypeStruct(((TOP_K * t + tm) * sub, LANES), F32),
        grid_spec=grid_spec,
        compiler_params=pltpu.CompilerParams(dimension_semantics=("arbitrary",), has_side_effects=True),
        name="moe_grouped",
    )(tile_e, n_used, tile_nv, src, dst, w_slot, h2_rows, wgu, bgu.reshape(-1, n_exp, 1, 2 * d), wd, bd.reshape(-1, n_exp, 1, d))


def _combine_kernel(x_ref, gt_ref, y0_ref, y1_ref, y2_ref, y3_ref, gf_ref, o_ref, y_sc, *, final):
    tm, d = x_ref.shape
    sub = d // LANES
    y = (y0_ref[...] + y1_ref[...]) + (y2_ref[...] + y3_ref[...])
    y_sc[...] = y
    y = jnp.concatenate([y_sc[pl.ds(c, tm, stride=sub), :] for c in range(sub)], axis=1)
    x = x_ref[...] + gt_ref[...] * y
    if final:
        ms = jnp.mean(x * x, axis=-1, keepdims=True)
        x = x * lax.rsqrt(ms + NORM_EPS) * gf_ref[...]
    o_ref[...] = x


def _combine(x1, gate2, planes, g_final, seq, final):
    t, d = x1.shape
    tiles_per_seq = seq // TM_COMBINE
    nt = t // TM_COMBINE
    row = lambda i: (i, 0)
    plane = lambda k: (lambda i: (k * nt + i, 0))
    blk = (TM_COMBINE, d)
    pblk = (TM_COMBINE * (d // LANES), LANES)
    return pl.pallas_call(
        functools.partial(_combine_kernel, final=final),
        out_shape=jax.ShapeDtypeStruct((t, d), F32),
        grid=(nt,),
        in_specs=[
            pl.BlockSpec(blk, row),
            pl.BlockSpec((None, 1, d), lambda i: (i // tiles_per_seq, 0, 0)),
            pl.BlockSpec(pblk, plane(0)), pl.BlockSpec(pblk, plane(1)),
            pl.BlockSpec(pblk, plane(2)), pl.BlockSpec(pblk, plane(3)),
            pl.BlockSpec((1, d), lambda i: (0, 0)),
        ],
        out_specs=pl.BlockSpec(blk, row),
        scratch_shapes=[pltpu.VMEM(pblk, F32)],
        compiler_params=_params("arbitrary"),
        name="moe_combine",
    )(x1, gate2, planes, planes, planes, planes, g_final)


def _pack_w_in(w):
    sizes = (256, 256, 256, 128, 128, 256, GLA_GATE_RANK, 256, 256, 256, 256, 256, 256, 128, 128)
    offs = np.concatenate([[0], np.cumsum(sizes)])
    mq, mk, mv, gq, gk, gv, ga, gr, rq, rk, rv, rg, sq, sk, sv = (w[:, int(offs[n]):int(offs[n + 1])] for n in range(15))
    merge = w[:, int(offs[15]):]
    d = w.shape[0]
    rep = lambda kv: jnp.repeat(kv.reshape(d, SWA_KV_HEADS, HEAD_DIM), BRANCH_HEADS // SWA_KV_HEADS, axis=1).reshape(d, BRANCH_WIDTH)
    scale = HEAD_DIM ** -0.5
    ga_pad = jnp.pad(ga, ((0, 0), (0, LANES - GLA_GATE_RANK)))
    packed = jnp.concatenate([mq * scale, mk, mv, gq, gk, gv, ga_pad, gr, rq, rk * scale, rv, rg,
                              sq * scale, rep(sk), rep(sv)], axis=1)
    return packed.astype(BF16), merge.astype(BF16)


def kernel(x, c, w_ada, b_ada, g_norm_mix, w_in, w_gla_gate, b_gla_gate, g_gla_norm, g_ret_norm, attn_sinks,
           w_branch, w_out, g_norm_ffn, w_router, b_router, w_gate_up, b_gate_up, w_down, b_down, g_final):
    batch, seq, d = x.shape
    depth = w_ada.shape[0]
    t = batch * seq
    n_alibi = 2 * BRANCH_HEADS
    slopes = [2.0 ** (-(k + 1.0) * (8.0 / n_alibi)) for k in range(n_alibi)]
    swa_slopes, moba_slopes = tuple(slopes[:BRANCH_HEADS]), tuple(slopes[BRANCH_HEADS:])
    log_gamma = jnp.log(1.0 - 2.0 ** (-RET_DECAY_BASE - jnp.arange(BRANCH_HEADS, dtype=F32)))
    log_gamma_row = jnp.repeat(log_gamma, HEAD_DIM).reshape(1, BRANCH_WIDTH)
    dummy_row = jnp.zeros((1, BRANCH_HEADS * GLA_KEY_DIM), F32)
    n_tiles = (TOP_K * t + N_EXPERTS * (TM_EXPERT - 1) + TM_EXPERT - 1) // TM_EXPERT

    mod = _modulation(c, w_ada, b_ada)
    xf = x.reshape(t, d)
    for l in range(depth):
        shift1, scale1, gate1, shift2, scale2, gate2 = (
            mod[l, :, n * d:(n + 1) * d].reshape(batch, 1, d) for n in range(6))
        w_packed, w_merge = _pack_w_in(w_in[l])
        wg = jnp.pad(w_gla_gate[l], ((0, LANES - GLA_GATE_RANK), (0, 0)))
        moba_qkv, gla_f, gla_v, ret_f, ret_v, swa_qkv, kmean = _input_projection(
            xf, scale1, shift1, g_norm_mix[l].reshape(1, d), w_packed, wg, b_gla_gate[l].reshape(1, -1), seq)
        kmean = kmean.reshape(batch, seq // MOBA_BLOCK, BRANCH_WIDTH)

        y_moba = _moba(moba_qkv, kmean, batch, seq, moba_slopes)
        y_gla = _linear_attention(gla_f, gla_v, g_gla_norm[l].reshape(1, -1), dummy_row, batch, seq,
                                  dk=GLA_KEY_DIM, gated=True, qoff=0, koff=128, laoff=256, roff=384,
                                  norm="rms", name="gla")
        y_ret = _linear_attention(ret_f, ret_v, g_ret_norm[l].reshape(1, -1), log_gamma_row, batch, seq,
                                  dk=HEAD_DIM, gated=False, qoff=0, koff=256, laoff=0, roff=512,
                                  norm="group", name="retention")
        sinks_row = jnp.pad(attn_sinks[l].reshape(1, -1), ((0, 0), (0, LANES - BRANCH_HEADS)))
        y_swa = _swa(swa_qkv, sinks_row, batch, seq, swa_slopes)

        wr = jnp.pad(w_router[l], ((0, 0), (0, LANES - N_EXPERTS)))
        br = jnp.pad(b_router[l].reshape(1, -1), ((0, 0), (0, LANES - N_EXPERTS)), constant_values=NEG_BIG)
        x1, h2, top_i, top_w = _merge(
            xf, scale1, shift1, gate1, g_norm_mix[l].reshape(1, d), (y_moba, y_gla, y_ret, y_swa),
            w_merge, w_branch[l].astype(BF16), w_out[l].astype(BF16),
            g_norm_ffn[l].reshape(1, d), scale2, shift2, wr, br, seq)

        tile_e, n_used, tile_nv, src, dst, w_slot = _route(top_i[:, :TOP_K], top_w[:, :TOP_K], t, n_tiles)
        planes = _moe(h2, tile_e, n_used, tile_nv, src, dst, w_slot, w_gate_up, b_gate_up, w_down, b_down, l)
        xf = _combine(x1, gate2, planes, g_final.reshape(1, d), seq, final=(l == depth - 1))
    return xf.reshape(batch, seq, d)
```

```python
import functools

import jax
import jax.numpy as jnp
import numpy as np
from jax import lax
from jax.experimental import pallas as pl
from jax.experimental.pallas import tpu as pltpu

F32 = jnp.float32
BF16 = jnp.bfloat16
I32 = jnp.int32
HI = lax.Precision.HIGHEST

HEAD_DIM = 64
N_BRANCHES = 4
BRANCH_HEADS = 4
BRANCH_WIDTH = HEAD_DIM * BRANCH_HEADS
MOBA_BLOCK = 256
MOBA_TOPK = 3
GLA_KEY_DIM = 32
GLA_GATE_RANK = 16
GLA_GATE_TEMP = 16.0
LINEAR_CHUNK = 64
RET_DECAY_BASE = 5.0
SWA_KV_HEADS = 2
WINDOW = 128
N_EXPERTS = 32
TOP_K = 4
SWIGLU_ALPHA = 1.702
SWIGLU_LIMIT = 7.0
NORM_EPS = 1e-5

LANES = 128
TM_IN = 512
TM_MERGE = 512
TM_EXPERT = 256
MOE_COL_CHUNK = 256
ROW_GROUP = 8
TM_COMBINE = 512
LIN_GROUP = 256
MOD_COLS = 1536
NEG_BIG = -1e30

_C_MOBA = (0, 768)
_C_GLA = (768, 1664)
_C_RET = (1664, 2688)
_C_SWA = (2688, 3456)
N_PACKED = 3456


def _dot(a, b, precision=None):
    return jnp.dot(a, b, preferred_element_type=F32, precision=precision)


def _dot_nt(a, b, precision=None):
    return lax.dot_general(a, b, (((1,), (1,)), ((), ())), preferred_element_type=F32, precision=precision)


def _dot_tn(a, b, precision=None):
    return lax.dot_general(a, b, (((0,), (0,)), ((), ())), preferred_element_type=F32, precision=precision)


def _norm_mod(x, g, scale, shift):
    ms = jnp.mean(x * x, axis=-1, keepdims=True)
    y = x * lax.rsqrt(ms + NORM_EPS) * g
    return y * (1.0 + scale) + shift


def _log_sigmoid(x):
    return jnp.minimum(x, 0.0) - jnp.log1p(jnp.exp(-jnp.abs(x)))


def _params(*sem):
    return pltpu.CompilerParams(dimension_semantics=tuple(sem))


def _mod_kernel(c_ref, w_ref, b_ref, o_ref):
    c = c_ref[...]
    o_ref[...] = _dot(jax.nn.silu(c), w_ref[...], HI) + b_ref[...]


def _modulation(c, w_ada, b_ada):
    depth, d, n = w_ada.shape
    b = c.shape[0]
    return pl.pallas_call(
        _mod_kernel,
        out_shape=jax.ShapeDtypeStruct((depth, b, n), F32),
        grid=(depth, n // MOD_COLS),
        in_specs=[
            pl.BlockSpec((b, d), lambda l, j: (0, 0)),
            pl.BlockSpec((None, d, MOD_COLS), lambda l, j: (l, 0, j)),
            pl.BlockSpec((None, 1, MOD_COLS), lambda l, j: (l, 0, j)),
        ],
        out_specs=pl.BlockSpec((None, b, MOD_COLS), lambda l, j: (l, 0, j)),
        compiler_params=_params("arbitrary", "arbitrary"),
        name="adaln_mod",
    )(c, w_ada, b_ada.reshape(depth, 1, n))


def _in_kernel(x_ref, sc_ref, sh_ref, g_ref, w_ref, wg_ref, bg_ref,
               moba_ref, glaf_ref, glav_ref, retf_ref, retv_ref, swa_ref, kmean_ref):
    h = _norm_mod(x_ref[...], g_ref[...], sc_ref[...], sh_ref[...]).astype(BF16)

    zm = _dot(h, w_ref[:, _C_MOBA[0]:_C_MOBA[1]])
    moba_ref[...] = zm.astype(BF16)
    for blk in range(TM_IN // MOBA_BLOCK):
        kb = zm[blk * MOBA_BLOCK:(blk + 1) * MOBA_BLOCK, 256:512]
        kmean_ref[blk:blk + 1, :] = jnp.mean(kb, axis=0, keepdims=True)

    zg = _dot(h, w_ref[:, _C_GLA[0]:_C_GLA[1]])
    gate_logit = _dot(zg[:, 512:640], wg_ref[...], HI) + bg_ref[...]
    glaf_ref[:, 0:128] = zg[:, 0:128] * (GLA_KEY_DIM ** -0.5)
    glaf_ref[:, 128:256] = zg[:, 128:256]
    glaf_ref[:, 256:384] = _log_sigmoid(gate_logit) / GLA_GATE_TEMP
    glaf_ref[:, 384:640] = zg[:, 640:896]
    glav_ref[...] = zg[:, 256:512].astype(BF16)

    zr = _dot(h, w_ref[:, _C_RET[0]:_C_RET[1]])
    retf_ref[:, 0:512] = zr[:, 0:512]
    retf_ref[:, 512:768] = zr[:, 768:1024]
    retv_ref[...] = zr[:, 512:768].astype(BF16)

    swa_ref[...] = _dot(h, w_ref[:, _C_SWA[0]:_C_SWA[1]]).astype(BF16)


def _input_projection(x2, scale, shift, g, w_packed, wg, bg, seq):
    t, d = x2.shape
    tiles_per_seq = seq // TM_IN
    nblk = TM_IN // MOBA_BLOCK
    row = lambda i: (i, 0)
    per_batch = lambda i: (i // tiles_per_seq, 0, 0)
    const2 = lambda i: (0, 0)
    outs = pl.pallas_call(
        _in_kernel,
        out_shape=(
            jax.ShapeDtypeStruct((t, 768), BF16),
            jax.ShapeDtypeStruct((t, 640), F32),
            jax.ShapeDtypeStruct((t, 256), BF16),
            jax.ShapeDtypeStruct((t, 768), F32),
            jax.ShapeDtypeStruct((t, 256), BF16),
            jax.ShapeDtypeStruct((t, 768), BF16),
            jax.ShapeDtypeStruct((t // TM_IN, nblk, 256), F32),
        ),
        grid=(t // TM_IN,),
        in_specs=[
            pl.BlockSpec((TM_IN, d), row),
            pl.BlockSpec((None, 1, d), per_batch),
            pl.BlockSpec((None, 1, d), per_batch),
            pl.BlockSpec((1, d), const2),
            pl.BlockSpec((d, N_PACKED), const2),
            pl.BlockSpec((LANES, LANES), const2),
            pl.BlockSpec((1, LANES), const2),
        ],
        out_specs=(
            pl.BlockSpec((TM_IN, 768), row),
            pl.BlockSpec((TM_IN, 640), row),
            pl.BlockSpec((TM_IN, 256), row),
            pl.BlockSpec((TM_IN, 768), row),
            pl.BlockSpec((TM_IN, 256), row),
            pl.BlockSpec((TM_IN, 768), row),
            pl.BlockSpec((None, nblk, 256), lambda i: (i, 0, 0)),
        ),
        compiler_params=_params("arbitrary"),
        name="input_proj",
    )(x2, scale, shift, g, w_packed, wg, bg)
    return outs


def _moba_kernel(q_ref, k_ref, v_ref, km_ref, o_ref, sel_sc, m_sc, l_sc, acc_sc, *, slopes):
    i = pl.program_id(1)
    nb = MOBA_BLOCK
    n_blocks = km_ref.shape[0]
    q = q_ref[...]
    lane = lax.broadcasted_iota(I32, (1, BRANCH_WIDTH), 1)
    head_of_lane = lane // HEAD_DIM

    rows = lax.broadcasted_iota(I32, (LANES, BRANCH_WIDTH), 0)
    lanes_full = lax.broadcasted_iota(I32, (LANES, BRANCH_WIDTH), 1)
    km = km_ref[...]
    km_t = jnp.concatenate([km] * (LANES // n_blocks), axis=0)
    km_heads = jnp.where((rows // n_blocks) == (lanes_full // HEAD_DIM), km_t, 0.0)
    gate_t = _dot_nt(km_heads, q.astype(F32), HI)

    jrow = lax.broadcasted_iota(I32, (n_blocks, nb), 0)
    valid = jrow < i
    for h in range(BRANCH_HEADS):
        g = jnp.where(valid, gate_t[h * n_blocks:(h + 1) * n_blocks, :], -jnp.inf)
        cnt = jnp.zeros((n_blocks, nb), F32)
        for jp in range(n_blocks):
            gj = g[jp:jp + 1, :]
            beats = (gj > g) | ((gj == g) & (jp < jrow))
            cnt = cnt + beats.astype(F32)
        sel_sc[h * n_blocks:(h + 1) * n_blocks, :] = ((cnt < MOBA_TOPK) & valid).astype(F32)

    key_io = lax.broadcasted_iota(I32, (nb, nb), 0)
    qry_io = lax.broadcasted_iota(I32, (nb, nb), 1)
    rel = (qry_io - key_io).astype(F32)

    m_sc[...] = jnp.full(m_sc.shape, -jnp.inf, F32)
    l_sc[...] = jnp.zeros(l_sc.shape, F32)
    acc_sc[...] = jnp.zeros(acc_sc.shape, F32)

    def process(j, own):
        start = pl.multiple_of(j * nb, nb)
        kj = k_ref[pl.ds(start, nb), :]
        vj_t = v_ref[pl.ds(start, nb), :].astype(F32).T
        dist = rel + ((i - j) * nb).astype(F32)
        for h in range(BRANCH_HEADS):
            hm = head_of_lane == h
            s = _dot_nt(kj, jnp.where(hm, q, jnp.zeros_like(q))) - slopes[h] * dist
            if own:
                mask = rel >= 0.0
            else:
                mask = sel_sc[pl.ds(h * n_blocks + j, 1), :] > 0.5
            s = jnp.where(mask, s, -jnp.inf)
            m_old = m_sc[h]
            m_new = jnp.maximum(m_old, jnp.max(s, axis=0, keepdims=True))
            alpha = jnp.exp(m_old - m_new)
            p = jnp.exp(s - m_new)
            l_sc[h] = alpha * l_sc[h] + jnp.sum(p, axis=0, keepdims=True)
            m_sc[h] = m_new
            hrows = slice(h * HEAD_DIM, (h + 1) * HEAD_DIM)
            pv = _dot(vj_t[hrows, :].astype(BF16), p.astype(BF16))
            acc_sc[hrows, :] = acc_sc[hrows, :] * alpha + pv

    process(i, True)

    def body(j, carry):
        process(j, False)
        return carry

    lax.fori_loop(0, i, body, 0)

    for h in range(BRANCH_HEADS):
        hrows = slice(h * HEAD_DIM, (h + 1) * HEAD_DIM)
        acc_sc[hrows, :] = acc_sc[hrows, :] * (1.0 / l_sc[h])
    o_ref[...] = acc_sc[...].T.astype(BF16)


def _moba(qkv, kmean, batch, seq, slopes):
    t = qkv.shape[0]
    nq = seq // MOBA_BLOCK
    kern = functools.partial(_moba_kernel, slopes=slopes)
    return pl.pallas_call(
        kern,
        out_shape=jax.ShapeDtypeStruct((t, BRANCH_WIDTH), BF16),
        grid=(batch, nq),
        in_specs=[
            pl.BlockSpec((MOBA_BLOCK, BRANCH_WIDTH), lambda b, i: (b * nq + i, 0)),
            pl.BlockSpec((seq, BRANCH_WIDTH), lambda b, i: (b, 1)),
            pl.BlockSpec((seq, BRANCH_WIDTH), lambda b, i: (b, 2)),
            pl.BlockSpec((None, nq, BRANCH_WIDTH), lambda b, i: (b, 0, 0)),
        ],
        out_specs=pl.BlockSpec((MOBA_BLOCK, BRANCH_WIDTH), lambda b, i: (b * nq + i, 0)),
        scratch_shapes=[
            pltpu.VMEM((BRANCH_HEADS * nq, MOBA_BLOCK), F32),
            pltpu.VMEM((BRANCH_HEADS, 1, MOBA_BLOCK), F32),
            pltpu.VMEM((BRANCH_HEADS, 1, MOBA_BLOCK), F32),
            pltpu.VMEM((BRANCH_WIDTH, MOBA_BLOCK), F32),
        ],
        compiler_params=_params("arbitrary", "arbitrary"),
        name="moba_attn",
    )(qkv, qkv, qkv, kmean)


def _swa_kernel(q_ref, kp_ref, kc_ref, vp_ref, vc_ref, sink_ref, o_ref, *, slopes):
    n = pl.program_id(1)
    w = WINDOW
    q = q_ref[...]
    kp, kc, vp, vc = kp_ref[...], kc_ref[...], vp_ref[...], vc_ref[...]
    head_of_lane = lax.broadcasted_iota(I32, (1, BRANCH_WIDTH), 1) // HEAD_DIM
    key_io = lax.broadcasted_iota(I32, (w, w), 0)
    qry_io = lax.broadcasted_iota(I32, (w, w), 1)
    rel = (qry_io - key_io).astype(F32)
    allow_cur = rel >= 0.0
    allow_prev = (rel < 0.0) & (n > 0)
    zq = jnp.zeros_like(q)
    vp_t = vp.astype(F32).T
    vc_t = vc.astype(F32).T
    outs = []
    for h in range(BRANCH_HEADS):
        qh = jnp.where(head_of_lane == h, q, zq)
        sp = jnp.where(allow_prev, _dot_nt(kp, qh) - slopes[h] * (rel + float(w)), -jnp.inf)
        sc = jnp.where(allow_cur, _dot_nt(kc, qh) - slopes[h] * rel, -jnp.inf)
        sink = sink_ref[0:1, h:h + 1]
        m = jnp.maximum(jnp.maximum(jnp.max(sp, axis=0, keepdims=True), jnp.max(sc, axis=0, keepdims=True)), sink)
        pp = jnp.exp(sp - m)
        pc = jnp.exp(sc - m)
        l = jnp.sum(pp, axis=0, keepdims=True) + jnp.sum(pc, axis=0, keepdims=True) + jnp.exp(sink - m)
        hrows = slice(h * HEAD_DIM, (h + 1) * HEAD_DIM)
        o = _dot(vp_t[hrows, :].astype(BF16), pp.astype(BF16)) + _dot(vc_t[hrows, :].astype(BF16), pc.astype(BF16))
        outs.append(o * (1.0 / l))
    o_ref[...] = jnp.concatenate(outs, axis=0).T.astype(BF16)


def _swa(qkv, sinks_row, batch, seq, slopes):
    t = qkv.shape[0]
    nq = seq // WINDOW
    kern = functools.partial(_swa_kernel, slopes=slopes)
    cur = lambda col: (lambda b, n: (b * nq + n, col))
    prev = lambda col: (lambda b, n: (b * nq + jnp.maximum(n - 1, 0), col))
    blk = (WINDOW, BRANCH_WIDTH)
    return pl.pallas_call(
        kern,
        out_shape=jax.ShapeDtypeStruct((t, BRANCH_WIDTH), BF16),
        grid=(batch, nq),
        in_specs=[
            pl.BlockSpec(blk, cur(0)),
            pl.BlockSpec(blk, prev(1)),
            pl.BlockSpec(blk, cur(1)),
            pl.BlockSpec(blk, prev(2)),
            pl.BlockSpec(blk, cur(2)),
            pl.BlockSpec((1, LANES), lambda b, n: (0, 0)),
        ],
        out_specs=pl.BlockSpec(blk, cur(0)),
        compiler_params=_params("arbitrary", "arbitrary"),
        name="swa_attn",
    )(qkv, qkv, qkv, qkv, qkv, sinks_row)


def _lin_kernel(f_ref, v_ref, g_ref, lg_ref, o_ref, st_sc, *, dk, gated, qoff, koff, laoff, roff, norm):
    width = BRANCH_HEADS * dk
    grp = LIN_GROUP
    ch = LINEAR_CHUNK
    seq = f_ref.shape[0]

    r_io = lax.broadcasted_iota(I32, (grp, grp), 0)
    c_io = lax.broadcasted_iota(I32, (grp, grp), 1)
    same = (r_io // ch) == (c_io // ch)
    ltri = (same & (c_io <= r_io)).astype(F32)
    lfull = same.astype(F32)
    head_avg = ((r_io // HEAD_DIM) == (c_io // HEAD_DIM)).astype(F32) * (1.0 / HEAD_DIM)
    lane_k = lax.broadcasted_iota(I32, (1, width), 1) // dk
    lane_v = lax.broadcasted_iota(I32, (1, BRANCH_WIDTH), 1) // HEAD_DIM
    bd_mask = (lax.broadcasted_iota(I32, (BRANCH_WIDTH, width), 0) // HEAD_DIM
               == lax.broadcasted_iota(I32, (BRANCH_WIDTH, width), 1) // dk)
    gain = g_ref[...]

    st_sc[...] = jnp.zeros(st_sc.shape, F32)

    def group(gi, carry):
        r0 = pl.multiple_of(gi * grp, grp)
        rows = pl.ds(r0, grp)
        q = f_ref[rows, qoff:qoff + width]
        k = f_ref[rows, koff:koff + width]
        if gated:
            la = f_ref[rows, laoff:laoff + width]
        else:
            la = jnp.broadcast_to(lg_ref[...], (grp, width))
        b = _dot(ltri, la, HI)
        be = _dot(lfull, la, HI)
        qd = (q * jnp.exp(b)).astype(BF16)
        kin = (k * jnp.exp(-b)).astype(BF16)
        kend = (k * jnp.exp(be - b)).astype(BF16)
        dec = jnp.exp(be)
        v = v_ref[rows, :]
        zq = jnp.zeros_like(qd)
        zv = jnp.zeros_like(v)

        o = jnp.zeros((grp, BRANCH_WIDTH), F32)
        for h in range(BRANCH_HEADS):
            a = _dot_nt(jnp.where(lane_k == h, qd, zq), kin) * ltri
            o = o + _dot(a.astype(BF16), jnp.where(lane_v == h, v, zv))

        st = st_sc[...]
        parts = []
        for c in range(grp // ch):
            sl = slice(c * ch, (c + 1) * ch)
            parts.append(_dot_nt(qd[sl], st.astype(BF16)))
            kv_t = _dot_tn(v[sl], kend[sl])
            st = st * dec[c * ch:c * ch + 1, :] + jnp.where(bd_mask, kv_t, 0.0)
        st_sc[...] = st
        o = o + jnp.concatenate(parts, axis=0)

        if norm == "rms":
            ms = _dot(o * o, head_avg, HI)
            y = o * lax.rsqrt(ms + NORM_EPS) * gain
        else:
            mu = _dot(o, head_avg, HI)
            xc = o - mu
            var = _dot(xc * xc, head_avg, HI)
            y = xc * lax.rsqrt(var + NORM_EPS) * gain
        r = f_ref[rows, roff:roff + BRANCH_WIDTH]
        o_ref[rows, :] = (y * jax.nn.silu(r)).astype(BF16)
        return carry

    lax.fori_loop(0, seq // grp, group, 0)


def _linear_attention(feat, v, gain, log_gamma_row, batch, seq, *, dk, gated, qoff, koff, laoff, roff, norm, name):
    t, fw = feat.shape
    width = BRANCH_HEADS * dk
    kern = functools.partial(_lin_kernel, dk=dk, gated=gated, qoff=qoff, koff=koff, laoff=laoff, roff=roff, norm=norm)
    return pl.pallas_call(
        kern,
        out_shape=jax.ShapeDtypeStruct((t, BRANCH_WIDTH), BF16),
        grid=(batch,),
        in_specs=[
            pl.BlockSpec((seq, fw), lambda b: (b, 0)),
            pl.BlockSpec((seq, BRANCH_WIDTH), lambda b: (b, 0)),
            pl.BlockSpec((1, BRANCH_WIDTH), lambda b: (0, 0)),
            pl.BlockSpec((1, width), lambda b: (0, 0)),
        ],
        out_specs=pl.BlockSpec((seq, BRANCH_WIDTH), lambda b: (b, 0)),
        scratch_shapes=[pltpu.VMEM((BRANCH_WIDTH, width), F32)],
        compiler_params=_params("arbitrary"),
        name=name,
    )(feat, v, gain, log_gamma_row)


def _merge_kernel(x_ref, sc1_ref, sh1_ref, gt1_ref, gnm_ref, ym_ref, yg_ref, yr_ref, ys_ref,
                  wm_ref, wb_ref, wo_ref, gnf_ref, sc2_ref, sh2_ref, wr_ref, br_ref,
                  x1_ref, h2_ref, ti_ref, tw_ref):
    x = x_ref[...]
    d = x.shape[1]
    hb = _norm_mod(x, gnm_ref[...], sc1_ref[...], sh1_ref[...]).astype(BF16)
    mixed = jnp.zeros(x.shape, F32)
    for n, y_ref in enumerate((ym_ref, yg_ref, yr_ref, ys_ref)):
        gate = jax.nn.sigmoid(_dot(hb, wm_ref[:, n * d:(n + 1) * d]))
        mixed = mixed + gate * _dot(y_ref[...], wb_ref[n])
    x1 = x + gt1_ref[...] * _dot(mixed.astype(BF16), wo_ref[...])
    x1_ref[...] = x1

    h2 = _norm_mod(x1, gnf_ref[...], sc2_ref[...], sh2_ref[...])
    sub = d // LANES
    for c in range(sub):
        h2_ref[pl.ds(c, x.shape[0], stride=sub), :] = h2[:, c * LANES:(c + 1) * LANES]
    logits = _dot(h2, wr_ref[...], HI) + br_ref[...]
    lane = lax.broadcasted_iota(I32, logits.shape, 1)
    vals, idxs = [], []
    cur = logits
    for _ in range(TOP_K):
        m = jnp.max(cur, axis=1, keepdims=True)
        idx = jnp.min(jnp.where(cur == m, lane, LANES), axis=1, keepdims=True)
        vals.append(m)
        idxs.append(idx)
        cur = jnp.where(lane == idx, -jnp.inf, cur)
    es = [jnp.exp(v - vals[0]) for v in vals]
    tot = es[0] + es[1] + es[2] + es[3]
    ti = jnp.zeros(logits.shape, I32)
    tw = jnp.zeros(logits.shape, F32)
    for k in range(TOP_K):
        ti = jnp.where(lane == k, idxs[k], ti)
        tw = jnp.where(lane == k, es[k] / tot, tw)
    ti_ref[...] = ti
    tw_ref[...] = tw


def _merge(x2, sc1, sh1, gt1, gnm, ys, wm, wb, wo, gnf, sc2, sh2, wr, br, seq):
    t, d = x2.shape
    tiles_per_seq = seq // TM_MERGE
    row = lambda i: (i, 0)
    per_batch = lambda i: (i // tiles_per_seq, 0, 0)
    c2 = lambda i: (0, 0)
    c3 = lambda i: (0, 0, 0)
    vec = pl.BlockSpec((None, 1, d), per_batch)
    ytile = pl.BlockSpec((TM_MERGE, BRANCH_WIDTH), row)
    return pl.pallas_call(
        _merge_kernel,
        out_shape=(
            jax.ShapeDtypeStruct((t, d), F32),
            jax.ShapeDtypeStruct((t * (d // LANES), LANES), F32),
            jax.ShapeDtypeStruct((t, LANES), I32),
            jax.ShapeDtypeStruct((t, LANES), F32),
        ),
        grid=(t // TM_MERGE,),
        in_specs=[
            pl.BlockSpec((TM_MERGE, d), row), vec, vec, vec,
            pl.BlockSpec((1, d), c2),
            ytile, ytile, ytile, ytile,
            pl.BlockSpec((d, N_BRANCHES * d), c2),
            pl.BlockSpec((N_BRANCHES, BRANCH_WIDTH, d), c3),
            pl.BlockSpec((d, d), c2),
            pl.BlockSpec((1, d), c2), vec, vec,
            pl.BlockSpec((d, LANES), c2),
            pl.BlockSpec((1, LANES), c2),
        ],
        out_specs=(
            pl.BlockSpec((TM_MERGE, d), row),
            pl.BlockSpec((TM_MERGE * (d // LANES), LANES), row),
            pl.BlockSpec((TM_MERGE, LANES), row),
            pl.BlockSpec((TM_MERGE, LANES), row),
        ),
        compiler_params=_params("arbitrary"),
        name="merge_router",
    )(x2, sc1, sh1, gt1, gnm, *ys, wm, wb, wo, gnf, sc2, sh2, wr, br)


def _route(top_i, top_w, n_tok, n_tiles):
    tm = TM_EXPERT
    flat_e = top_i.reshape(-1)
    flat_w = top_w.reshape(-1)
    n_assign = flat_e.shape[0]
    sorted_e, order, sorted_w = lax.sort((flat_e, jnp.arange(n_assign, dtype=I32), flat_w), num_keys=1, is_stable=True)
    counts = jnp.sum((flat_e[:, None] == jnp.arange(N_EXPERTS, dtype=I32)[None, :]).astype(I32), axis=0)
    bounds = jnp.concatenate([jnp.zeros((1,), I32), jnp.cumsum(counts).astype(I32)])
    padded = ((counts + tm - 1) // tm) * tm
    gend = jnp.cumsum(padded)
    gstart = gend - padded
    tile_start = jnp.arange(n_tiles, dtype=I32) * tm
    tile_e_raw = jnp.sum((tile_start[:, None] >= gend[None, :]).astype(I32), axis=1)
    tile_e = jnp.minimum(tile_e_raw, N_EXPERTS - 1).astype(I32)
    n_used = (gend[-1] // tm).astype(I32).reshape(1)
    t_off = tile_start - gstart[tile_e]
    t_cnt = jnp.where(tile_e_raw < N_EXPERTS, counts[tile_e], 0)
    row = jnp.arange(tm, dtype=I32)[None, :]
    j = t_off[:, None] + row
    valid = j < t_cnt[:, None]
    p = jnp.clip(bounds[tile_e][:, None] + j, 0, n_assign - 1)
    a = order[p]
    tok = a // TOP_K
    kk = a % TOP_K
    src = jnp.where(valid, tok, 0).astype(I32)
    dst = jnp.where(valid, kk * n_tok + tok, TOP_K * n_tok + row).astype(I32)
    w = jnp.where(valid, sorted_w[p], 0.0).astype(F32)
    tile_nv = jnp.clip(t_cnt - t_off, 0, tm).astype(I32)
    return tile_e, n_used, tile_nv, src.reshape(-1), dst.reshape(-1), w.reshape(-1, 1)


def _moe_kernel(te_ref, nu_ref, nv_ref, src_hbm, dst_hbm, w_ref, h_hbm, wgu_ref, bgu_ref, wd_ref, bd_ref,
                y_hbm, src_s, dst_s, xs, ys, wgu_b, wd_b, isem, gsem, ssem):
    i = pl.program_id(0)
    tm = TM_EXPERT
    sub = xs.shape[1] // tm
    d = sub * LANES
    n_used = nu_ref[0]

    def tok_rows(t):
        return pl.ds(pl.multiple_of(t * sub, sub), sub)

    def idx_copies(tile, buf):
        base = pl.multiple_of(tile * tm, tm)
        return (pltpu.make_async_copy(src_hbm.at[pl.ds(base, tm)], src_s.at[buf], isem.at[buf, 0]),
                pltpu.make_async_copy(dst_hbm.at[pl.ds(base, tm)], dst_s.at[buf], isem.at[buf, 1]))

    def start_idx(tile, buf):
        for cp in idx_copies(tile, buf):
            cp.start()

    def wait_idx(tile, buf):
        for cp in idx_copies(tile, buf):
            cp.wait()

    def groups(tile):
        return lax.shift_right_logical(nv_ref[tile] + (ROW_GROUP - 1), ROW_GROUP.bit_length() - 1)

    def moved_rows(tile):
        return pl.multiple_of(groups(tile) * (ROW_GROUP * sub), ROW_GROUP * sub)

    def issue_gather(buf, slot):
        def body(r, carry):
            pltpu.make_async_copy(h_hbm.at[tok_rows(src_s[buf, r])], xs.at[slot, tok_rows(r)], gsem.at[slot]).start()
            return carry
        lax.fori_loop(0, tm, body, 0, unroll=8)

    def wait_gather(slot):
        pltpu.make_async_copy(h_hbm.at[pl.ds(0, tm * sub)], xs.at[slot], gsem.at[slot]).wait()

    def wait_scatter(tile):
        n = moved_rows(tile)
        pltpu.make_async_copy(ys.at[pl.ds(0, n)], y_hbm.at[pl.ds(0, n)], ssem.at[0]).wait()

    @pl.when(i == 0)
    def _():
        ys[...] = jnp.zeros(ys.shape, F32)
        fill = pltpu.make_async_copy(ys, y_hbm.at[pl.ds(y_hbm.shape[0] - tm * sub, tm * sub)], ssem.at[0])
        fill.start()
        fill.wait()

        @pl.when(n_used > 0)
        def _():
            start_idx(0, 0)
            wait_idx(0, 0)
            issue_gather(0, 0)

        @pl.when(n_used > 1)
        def _():
            start_idx(1, 1)

    @pl.when(i < n_used)
    def _():
        slot = lax.rem(i, 2)
        buf = lax.rem(i, 3)

        nxt = jnp.minimum(i + 1, n_used - 1)
        nbuf = lax.rem(nxt, 3)

        @pl.when(i + 1 < n_used)
        def _():
            wait_idx(i + 1, nbuf)

        @pl.when(i + 2 < n_used)
        def _():
            start_idx(i + 2, lax.rem(i + 2, 3))

        @pl.when((i == 0) | (te_ref[i] != te_ref[jnp.maximum(i - 1, 0)]))
        def _():
            rows = d // 8

            def cast(c, carry):
                r = pl.ds(pl.multiple_of(c * rows, rows), rows)
                wgu_b[r, :] = wgu_ref[r, :].astype(BF16)
                wd_b[r, :] = wd_ref[r, :].astype(BF16)
                return carry
            lax.fori_loop(0, 8, cast, 0)

        wait_gather(slot)
        x = jnp.concatenate([xs[slot, pl.ds(c, tm, stride=sub), :] for c in range(sub)], axis=1).astype(BF16)
        acts = []
        n_chunk = d // MOE_COL_CHUNK
        rows_per_chunk = tm // n_chunk
        for c in range(n_chunk):
            for r in range(c * rows_per_chunk, (c + 1) * rows_per_chunk):
                pltpu.make_async_copy(h_hbm.at[tok_rows(src_s[nbuf, r])], xs.at[1 - slot, tok_rows(r)],
                                      gsem.at[1 - slot]).start()
            cg = slice(c * MOE_COL_CHUNK, (c + 1) * MOE_COL_CHUNK)
            cl = slice(d + c * MOE_COL_CHUNK, d + (c + 1) * MOE_COL_CHUNK)
            x_glu = jnp.minimum(_dot(x, wgu_b[:, cg]) + bgu_ref[:, cg], SWIGLU_LIMIT)
            x_lin = jnp.clip(_dot(x, wgu_b[:, cl]) + bgu_ref[:, cl], -SWIGLU_LIMIT, SWIGLU_LIMIT)
            acts.append((x_glu * jax.nn.sigmoid(SWIGLU_ALPHA * x_glu) * (x_lin + 1.0)).astype(BF16))
        act = jnp.concatenate(acts, axis=1)
        out = (_dot(act, wd_b[...]) + bd_ref[...]) * w_ref[...]

        @pl.when(i > 0)
        def _():
            wait_scatter(i - 1)

        for c in range(sub):
            ys[pl.ds(c, tm, stride=sub), :] = out[:, c * LANES:(c + 1) * LANES]

        def scatter(g, carry):
            for u in range(ROW_GROUP):
                r = g * ROW_GROUP + u
                pltpu.make_async_copy(ys.at[tok_rows(r)], y_hbm.at[tok_rows(dst_s[buf, r])], ssem.at[0]).start()
            return carry
        lax.fori_loop(0, groups(i), scatter, 0)

        @pl.when(i == n_used - 1)
        def _():
            wait_scatter(i)
            wait_gather(1 - slot)


def _moe(h2_rows, tile_e, n_used, tile_nv, src, dst, w_slot, wgu, bgu, wd, bd, layer):
    d = wgu.shape[2]
    sub = d // LANES
    t = h2_rows.shape[0] // sub
    n_exp = wgu.shape[1]
    n_tiles = tile_e.shape[0]
    tm = TM_EXPERT
    any_spec = pl.BlockSpec(memory_space=pl.ANY)
    expert = lambda i, te, nu, nv: (layer, te[i], 0, 0)
    grid_spec = pltpu.PrefetchScalarGridSpec(
        num_scalar_prefetch=3,
        grid=(n_tiles,),
        in_specs=[
            any_spec, any_spec,
            pl.BlockSpec((tm, 1), lambda i, te, nu, nv: (i, 0)),
            any_spec,
            pl.BlockSpec((None, None, d, 2 * d), expert),
            pl.BlockSpec((None, None, 1, 2 * d), expert),
            pl.BlockSpec((None, None, d, d), expert),
            pl.BlockSpec((None, None, 1, d), expert),
        ],
        out_specs=any_spec,
        scratch_shapes=[
            pltpu.SMEM((3, tm), I32),
            pltpu.SMEM((3, tm), I32),
            pltpu.VMEM((2, tm * sub, LANES), F32),
            pltpu.VMEM((tm * sub, LANES), F32),
            pltpu.VMEM((d, 2 * d), BF16),
            pltpu.VMEM((d, d), BF16),
            pltpu.SemaphoreType.DMA((3, 2)),
            pltpu.SemaphoreType.DMA((2,)),
            pltpu.SemaphoreType.DMA((1,)),
        ],
    )
    return pl.pallas_call(
        _moe_kernel,
        out_shape=jax.ShapeDtypeStruct(((TOP_K * t + tm) * sub, LANES), F32),
        grid_spec=grid_spec,
        compiler_params=pltpu.CompilerParams(dimension_semantics=("arbitrary",), has_side_effects=True),
        name="moe_grouped",
    )(tile_e, n_used, tile_nv, src, dst, w_slot, h2_rows, wgu, bgu.reshape(-1, n_exp, 1, 2 * d), wd, bd.reshape(-1, n_exp, 1, d))


def _combine_kernel(x_ref, gt_ref, y0_ref, y1_ref, y2_ref, y3_ref, gf_ref, o_ref, y_sc, *, final):
    tm, d = x_ref.shape
    sub = d // LANES
    y = (y0_ref[...] + y1_ref[...]) + (y2_ref[...] + y3_ref[...])
    y_sc[...] = y
    y = jnp.concatenate([y_sc[pl.ds(c, tm, stride=sub), :] for c in range(sub)], axis=1)
    x = x_ref[...] + gt_ref[...] * y
    if final:
        ms = jnp.mean(x * x, axis=-1, keepdims=True)
        x = x * lax.rsqrt(ms + NORM_EPS) * gf_ref[...]
    o_ref[...] = x


def _combine(x1, gate2, planes, g_final, seq, final):
    t, d = x1.shape
    tiles_per_seq = seq // TM_COMBINE
    nt = t // TM_COMBINE
    row = lambda i: (i, 0)
    plane = lambda k: (lambda i: (k * nt + i, 0))
    blk = (TM_COMBINE, d)
    pblk = (TM_COMBINE * (d // LANES), LANES)
    return pl.pallas_call(
        functools.partial(_combine_kernel, final=final),
        out_shape=jax.ShapeDtypeStruct((t, d), F32),
        grid=(nt,),
        in_specs=[
            pl.BlockSpec(blk, row),
            pl.BlockSpec((None, 1, d), lambda i: (i // tiles_per_seq, 0, 0)),
            pl.BlockSpec(pblk, plane(0)), pl.BlockSpec(pblk, plane(1)),
            pl.BlockSpec(pblk, plane(2)), pl.BlockSpec(pblk, plane(3)),
            pl.BlockSpec((1, d), lambda i: (0, 0)),
        ],
        out_specs=pl.BlockSpec(blk, row),
        scratch_shapes=[pltpu.VMEM(pblk, F32)],
        compiler_params=_params("arbitrary"),
        name="moe_combine",
    )(x1, gate2, planes, planes, planes, planes, g_final)


def _pack_w_in(w):
    sizes = (256, 256, 256, 128, 128, 256, GLA_GATE_RANK, 256, 256, 256, 256, 256, 256, 128, 128)
    offs = np.concatenate([[0], np.cumsum(sizes)])
    mq, mk, mv, gq, gk, gv, ga, gr, rq, rk, rv, rg, sq, sk, sv = (w[:, int(offs[n]):int(offs[n + 1])] for n in range(15))
    merge = w[:, int(offs[15]):]
    d = w.shape[0]
    rep = lambda kv: jnp.repeat(kv.reshape(d, SWA_KV_HEADS, HEAD_DIM), BRANCH_HEADS // SWA_KV_HEADS, axis=1).reshape(d, BRANCH_WIDTH)
    scale = HEAD_DIM ** -0.5
    ga_pad = jnp.pad(ga, ((0, 0), (0, LANES - GLA_GATE_RANK)))
    packed = jnp.concatenate([mq * scale, mk, mv, gq, gk, gv, ga_pad, gr, rq, rk * scale, rv, rg,
                              sq * scale, rep(sk), rep(sv)], axis=1)
    return packed.astype(BF16), merge.astype(BF16)


def kernel(x, c, w_ada, b_ada, g_norm_mix, w_in, w_gla_gate, b_gla_gate, g_gla_norm, g_ret_norm, attn_sinks,
           w_branch, w_out, g_norm_ffn, w_router, b_router, w_gate_up, b_gate_up, w_down, b_down, g_final):
    batch, seq, d = x.shape
    depth = w_ada.shape[0]
    t = batch * seq
    n_alibi = 2 * BRANCH_HEADS
    slopes = [2.0 ** (-(k + 1.0) * (8.0 / n_alibi)) for k in range(n_alibi)]
    swa_slopes, moba_slopes = tuple(slopes[:BRANCH_HEADS]), tuple(slopes[BRANCH_HEADS:])
    log_gamma = jnp.log(1.0 - 2.0 ** (-RET_DECAY_BASE - jnp.arange(BRANCH_HEADS, dtype=F32)))
    log_gamma_row = jnp.repeat(log_gamma, HEAD_DIM).reshape(1, BRANCH_WIDTH)
    dummy_row = jnp.zeros((1, BRANCH_HEADS * GLA_KEY_DIM), F32)
    n_tiles = (TOP_K * t + N_EXPERTS * (TM_EXPERT - 1) + TM_EXPERT - 1) // TM_EXPERT

    mod = _modulation(c, w_ada, b_ada)
    xf = x.reshape(t, d)
    for l in range(depth):
        shift1, scale1, gate1, shift2, scale2, gate2 = (
            mod[l, :, n * d:(n + 1) * d].reshape(batch, 1, d) for n in range(6))
        w_packed, w_merge = _pack_w_in(w_in[l])
        wg = jnp.pad(w_gla_gate[l], ((0, LANES - GLA_GATE_RANK), (0, 0)))
        moba_qkv, gla_f, gla_v, ret_f, ret_v, swa_qkv, kmean = _input_projection(
            xf, scale1, shift1, g_norm_mix[l].reshape(1, d), w_packed, wg, b_gla_gate[l].reshape(1, -1), seq)
        kmean = kmean.reshape(batch, seq // MOBA_BLOCK, BRANCH_WIDTH)

        y_moba = _moba(moba_qkv, kmean, batch, seq, moba_slopes)
        y_gla = _linear_attention(gla_f, gla_v, g_gla_norm[l].reshape(1, -1), dummy_row, batch, seq,
                                  dk=GLA_KEY_DIM, gated=True, qoff=0, koff=128, laoff=256, roff=384,
                                  norm="rms", name="gla")
        y_ret = _linear_attention(ret_f, ret_v, g_ret_norm[l].reshape(1, -1), log_gamma_row, batch, seq,
                                  dk=HEAD_DIM, gated=False, qoff=0, koff=256, laoff=0, roff=512,
                                  norm="group", name="retention")
        sinks_row = jnp.pad(attn_sinks[l].reshape(1, -1), ((0, 0), (0, LANES - BRANCH_HEADS)))
        y_swa = _swa(swa_qkv, sinks_row, batch, seq, swa_slopes)

        wr = jnp.pad(w_router[l], ((0, 0), (0, LANES - N_EXPERTS)))
        br = jnp.pad(b_router[l].reshape(1, -1), ((0, 0), (0, LANES - N_EXPERTS)), constant_values=NEG_BIG)
        x1, h2, top_i, top_w = _merge(
            xf, scale1, shift1, gate1, g_norm_mix[l].reshape(1, d), (y_moba, y_gla, y_ret, y_swa),
            w_merge, w_branch[l].astype(BF16), w_out[l].astype(BF16),
            g_norm_ffn[l].reshape(1, d), scale2, shift2, wr, br, seq)

        tile_e, n_used, tile_nv, src, dst, w_slot = _route(top_i[:, :TOP_K], top_w[:, :TOP_K], t, n_tiles)
        planes = _moe(h2, tile_e, n_used, tile_nv, src, dst, w_slot, w_gate_up, b_gate_up, w_down, b_down, l)
        xf = _combine(x1, gate2, planes, g_final.reshape(1, d), seq, final=(l == depth - 1))
    return xf.reshape(batch, seq, d)
```

```python
import functools

import jax
import jax.numpy as jnp
import numpy as np
from jax import lax
from jax.experimental import pallas as pl
from jax.experimental.pallas import tpu as pltpu

F32 = jnp.float32
BF16 = jnp.bfloat16
I32 = jnp.int32
HI = lax.Precision.HIGHEST

HEAD_DIM = 64
N_BRANCHES = 4
BRANCH_HEADS = 4
BRANCH_WIDTH = HEAD_DIM * BRANCH_HEADS
MOBA_BLOCK = 256
MOBA_TOPK = 3
GLA_KEY_DIM = 32
GLA_GATE_RANK = 16
GLA_GATE_TEMP = 16.0
LINEAR_CHUNK = 64
RET_DECAY_BASE = 5.0
SWA_KV_HEADS = 2
WINDOW = 128
N_EXPERTS = 32
TOP_K = 4
SWIGLU_ALPHA = 1.702
SWIGLU_LIMIT = 7.0
NORM_EPS = 1e-5

LANES = 128
TM_IN = 512
TM_MERGE = 512
TM_EXPERT = 256
ROW_GROUP = 8
TM_COMBINE = 512
LIN_GROUP = 256
MOD_COLS = 1536
NEG_BIG = -1e30

_C_MOBA = (0, 768)
_C_GLA = (768, 1664)
_C_RET = (1664, 2688)
_C_SWA = (2688, 3456)
N_PACKED = 3456


def _dot(a, b, precision=None):
    return jnp.dot(a, b, preferred_element_type=F32, precision=precision)


def _dot_nt(a, b, precision=None):
    return lax.dot_general(a, b, (((1,), (1,)), ((), ())), preferred_element_type=F32, precision=precision)


def _dot_tn(a, b, precision=None):
    return lax.dot_general(a, b, (((0,), (0,)), ((), ())), preferred_element_type=F32, precision=precision)


def _norm_mod(x, g, scale, shift):
    ms = jnp.mean(x * x, axis=-1, keepdims=True)
    y = x * lax.rsqrt(ms + NORM_EPS) * g
    return y * (1.0 + scale) + shift


def _log_sigmoid(x):
    return jnp.minimum(x, 0.0) - jnp.log1p(jnp.exp(-jnp.abs(x)))


def _params(*sem):
    return pltpu.CompilerParams(dimension_semantics=tuple(sem))


def _mod_kernel(c_ref, w_ref, b_ref, o_ref):
    c = c_ref[...]
    o_ref[...] = _dot(jax.nn.silu(c), w_ref[...], HI) + b_ref[...]


def _modulation(c, w_ada, b_ada):
    depth, d, n = w_ada.shape
    b = c.shape[0]
    return pl.pallas_call(
        _mod_kernel,
        out_shape=jax.ShapeDtypeStruct((depth, b, n), F32),
        grid=(depth, n // MOD_COLS),
        in_specs=[
            pl.BlockSpec((b, d), lambda l, j: (0, 0)),
            pl.BlockSpec((None, d, MOD_COLS), lambda l, j: (l, 0, j)),
            pl.BlockSpec((None, 1, MOD_COLS), lambda l, j: (l, 0, j)),
        ],
        out_specs=pl.BlockSpec((None, b, MOD_COLS), lambda l, j: (l, 0, j)),
        compiler_params=_params("arbitrary", "arbitrary"),
        name="adaln_mod",
    )(c, w_ada, b_ada.reshape(depth, 1, n))


def _in_kernel(x_ref, sc_ref, sh_ref, g_ref, w_ref, wg_ref, bg_ref,
               moba_ref, glaf_ref, glav_ref, retf_ref, retv_ref, swa_ref, kmean_ref):
    h = _norm_mod(x_ref[...], g_ref[...], sc_ref[...], sh_ref[...]).astype(BF16)

    zm = _dot(h, w_ref[:, _C_MOBA[0]:_C_MOBA[1]])
    moba_ref[...] = zm.astype(BF16)
    for blk in range(TM_IN // MOBA_BLOCK):
        kb = zm[blk * MOBA_BLOCK:(blk + 1) * MOBA_BLOCK, 256:512]
        kmean_ref[blk:blk + 1, :] = jnp.mean(kb, axis=0, keepdims=True)

    zg = _dot(h, w_ref[:, _C_GLA[0]:_C_GLA[1]])
    gate_logit = _dot(zg[:, 512:640], wg_ref[...], HI) + bg_ref[...]
    glaf_ref[:, 0:128] = zg[:, 0:128] * (GLA_KEY_DIM ** -0.5)
    glaf_ref[:, 128:256] = zg[:, 128:256]
    glaf_ref[:, 256:384] = _log_sigmoid(gate_logit) / GLA_GATE_TEMP
    glaf_ref[:, 384:640] = zg[:, 640:896]
    glav_ref[...] = zg[:, 256:512].astype(BF16)

    zr = _dot(h, w_ref[:, _C_RET[0]:_C_RET[1]])
    retf_ref[:, 0:512] = zr[:, 0:512]
    retf_ref[:, 512:768] = zr[:, 768:1024]
    retv_ref[...] = zr[:, 512:768].astype(BF16)

    swa_ref[...] = _dot(h, w_ref[:, _C_SWA[0]:_C_SWA[1]]).astype(BF16)


def _input_projection(x2, scale, shift, g, w_packed, wg, bg, seq):
    t, d = x2.shape
    tiles_per_seq = seq // TM_IN
    nblk = TM_IN // MOBA_BLOCK
    row = lambda i: (i, 0)
    per_batch = lambda i: (i // tiles_per_seq, 0, 0)
    const2 = lambda i: (0, 0)
    outs = pl.pallas_call(
        _in_kernel,
        out_shape=(
            jax.ShapeDtypeStruct((t, 768), BF16),
            jax.ShapeDtypeStruct((t, 640), F32),
            jax.ShapeDtypeStruct((t, 256), BF16),
            jax.ShapeDtypeStruct((t, 768), F32),
            jax.ShapeDtypeStruct((t, 256), BF16),
            jax.ShapeDtypeStruct((t, 768), BF16),
            jax.ShapeDtypeStruct((t // TM_IN, nblk, 256), F32),
        ),
        grid=(t // TM_IN,),
        in_specs=[
            pl.BlockSpec((TM_IN, d), row),
            pl.BlockSpec((None, 1, d), per_batch),
            pl.BlockSpec((None, 1, d), per_batch),
            pl.BlockSpec((1, d), const2),
            pl.BlockSpec((d, N_PACKED), const2),
            pl.BlockSpec((LANES, LANES), const2),
            pl.BlockSpec((1, LANES), const2),
        ],
        out_specs=(
            pl.BlockSpec((TM_IN, 768), row),
            pl.BlockSpec((TM_IN, 640), row),
            pl.BlockSpec((TM_IN, 256), row),
            pl.BlockSpec((TM_IN, 768), row),
            pl.BlockSpec((TM_IN, 256), row),
            pl.BlockSpec((TM_IN, 768), row),
            pl.BlockSpec((None, nblk, 256), lambda i: (i, 0, 0)),
        ),
        compiler_params=_params("arbitrary"),
        name="input_proj",
    )(x2, scale, shift, g, w_packed, wg, bg)
    return outs


def _moba_kernel(q_ref, k_ref, v_ref, km_ref, o_ref, sel_sc, m_sc, l_sc, acc_sc, *, slopes):
    i = pl.program_id(1)
    nb = MOBA_BLOCK
    n_blocks = km_ref.shape[0]
    q = q_ref[...]
    lane = lax.broadcasted_iota(I32, (1, BRANCH_WIDTH), 1)
    head_of_lane = lane // HEAD_DIM

    rows = lax.broadcasted_iota(I32, (LANES, BRANCH_WIDTH), 0)
    lanes_full = lax.broadcasted_iota(I32, (LANES, BRANCH_WIDTH), 1)
    km = km_ref[...]
    km_t = jnp.concatenate([km] * (LANES // n_blocks), axis=0)
    km_heads = jnp.where((rows // n_blocks) == (lanes_full // HEAD_DIM), km_t, 0.0)
    gate_t = _dot_nt(km_heads, q.astype(F32), HI)

    jrow = lax.broadcasted_iota(I32, (n_blocks, nb), 0)
    valid = jrow < i
    for h in range(BRANCH_HEADS):
        g = jnp.where(valid, gate_t[h * n_blocks:(h + 1) * n_blocks, :], -jnp.inf)
        cnt = jnp.zeros((n_blocks, nb), F32)
        for jp in range(n_blocks):
            gj = g[jp:jp + 1, :]
            beats = (gj > g) | ((gj == g) & (jp < jrow))
            cnt = cnt + beats.astype(F32)
        sel_sc[h * n_blocks:(h + 1) * n_blocks, :] = ((cnt < MOBA_TOPK) & valid).astype(F32)

    key_io = lax.broadcasted_iota(I32, (nb, nb), 0)
    qry_io = lax.broadcasted_iota(I32, (nb, nb), 1)
    rel = (qry_io - key_io).astype(F32)

    m_sc[...] = jnp.full(m_sc.shape, -jnp.inf, F32)
    l_sc[...] = jnp.zeros(l_sc.shape, F32)
    acc_sc[...] = jnp.zeros(acc_sc.shape, F32)

    def process(j, own):
        start = pl.multiple_of(j * nb, nb)
        kj = k_ref[pl.ds(start, nb), :]
        vj_t = v_ref[pl.ds(start, nb), :].astype(F32).T
        dist = rel + ((i - j) * nb).astype(F32)
        for h in range(BRANCH_HEADS):
            hm = head_of_lane == h
            s = _dot_nt(kj, jnp.where(hm, q, jnp.zeros_like(q))) - slopes[h] * dist
            if own:
                mask = rel >= 0.0
            else:
                mask = sel_sc[pl.ds(h * n_blocks + j, 1), :] > 0.5
            s = jnp.where(mask, s, -jnp.inf)
            m_old = m_sc[h]
            m_new = jnp.maximum(m_old, jnp.max(s, axis=0, keepdims=True))
            alpha = jnp.exp(m_old - m_new)
            p = jnp.exp(s - m_new)
            l_sc[h] = alpha * l_sc[h] + jnp.sum(p, axis=0, keepdims=True)
            m_sc[h] = m_new
            hrows = slice(h * HEAD_DIM, (h + 1) * HEAD_DIM)
            pv = _dot(vj_t[hrows, :].astype(BF16), p.astype(BF16))
            acc_sc[hrows, :] = acc_sc[hrows, :] * alpha + pv

    process(i, True)

    def body(j, carry):
        process(j, False)
        return carry

    lax.fori_loop(0, i, body, 0)

    for h in range(BRANCH_HEADS):
        hrows = slice(h * HEAD_DIM, (h + 1) * HEAD_DIM)
        acc_sc[hrows, :] = acc_sc[hrows, :] * (1.0 / l_sc[h])
    o_ref[...] = acc_sc[...].T.astype(BF16)


def _moba(qkv, kmean, batch, seq, slopes):
    t = qkv.shape[0]
    nq = seq // MOBA_BLOCK
    kern = functools.partial(_moba_kernel, slopes=slopes)
    return pl.pallas_call(
        kern,
        out_shape=jax.ShapeDtypeStruct((t, BRANCH_WIDTH), BF16),
        grid=(batch, nq),
        in_specs=[
            pl.BlockSpec((MOBA_BLOCK, BRANCH_WIDTH), lambda b, i: (b * nq + i, 0)),
            pl.BlockSpec((seq, BRANCH_WIDTH), lambda b, i: (b, 1)),
            pl.BlockSpec((seq, BRANCH_WIDTH), lambda b, i: (b, 2)),
            pl.BlockSpec((None, nq, BRANCH_WIDTH), lambda b, i: (b, 0, 0)),
        ],
        out_specs=pl.BlockSpec((MOBA_BLOCK, BRANCH_WIDTH), lambda b, i: (b * nq + i, 0)),
        scratch_shapes=[
            pltpu.VMEM((BRANCH_HEADS * nq, MOBA_BLOCK), F32),
            pltpu.VMEM((BRANCH_HEADS, 1, MOBA_BLOCK), F32),
            pltpu.VMEM((BRANCH_HEADS, 1, MOBA_BLOCK), F32),
            pltpu.VMEM((BRANCH_WIDTH, MOBA_BLOCK), F32),
        ],
        compiler_params=_params("arbitrary", "arbitrary"),
        name="moba_attn",
    )(qkv, qkv, qkv, kmean)


def _swa_kernel(q_ref, kp_ref, kc_ref, vp_ref, vc_ref, sink_ref, o_ref, *, slopes):
    n = pl.program_id(1)
    w = WINDOW
    q = q_ref[...]
    kp, kc, vp, vc = kp_ref[...], kc_ref[...], vp_ref[...], vc_ref[...]
    head_of_lane = lax.broadcasted_iota(I32, (1, BRANCH_WIDTH), 1) // HEAD_DIM
    key_io = lax.broadcasted_iota(I32, (w, w), 0)
    qry_io = lax.broadcasted_iota(I32, (w, w), 1)
    rel = (qry_io - key_io).astype(F32)
    allow_cur = rel >= 0.0
    allow_prev = (rel < 0.0) & (n > 0)
    zq = jnp.zeros_like(q)
    vp_t = vp.astype(F32).T
    vc_t = vc.astype(F32).T
    outs = []
    for h in range(BRANCH_HEADS):
        qh = jnp.where(head_of_lane == h, q, zq)
        sp = jnp.where(allow_prev, _dot_nt(kp, qh) - slopes[h] * (rel + float(w)), -jnp.inf)
        sc = jnp.where(allow_cur, _dot_nt(kc, qh) - slopes[h] * rel, -jnp.inf)
        sink = sink_ref[0:1, h:h + 1]
        m = jnp.maximum(jnp.maximum(jnp.max(sp, axis=0, keepdims=True), jnp.max(sc, axis=0, keepdims=True)), sink)
        pp = jnp.exp(sp - m)
        pc = jnp.exp(sc - m)
        l = jnp.sum(pp, axis=0, keepdims=True) + jnp.sum(pc, axis=0, keepdims=True) + jnp.exp(sink - m)
        hrows = slice(h * HEAD_DIM, (h + 1) * HEAD_DIM)
        o = _dot(vp_t[hrows, :].astype(BF16), pp.astype(BF16)) + _dot(vc_t[hrows, :].astype(BF16), pc.astype(BF16))
        outs.append(o * (1.0 / l))
    o_ref[...] = jnp.concatenate(outs, axis=0).T.astype(BF16)


def _swa(qkv, sinks_row, batch, seq, slopes):
    t = qkv.shape[0]
    nq = seq // WINDOW
    kern = functools.partial(_swa_kernel, slopes=slopes)
    cur = lambda col: (lambda b, n: (b * nq + n, col))
    prev = lambda col: (lambda b, n: (b * nq + jnp.maximum(n - 1, 0), col))
    blk = (WINDOW, BRANCH_WIDTH)
    return pl.pallas_call(
        kern,
        out_shape=jax.ShapeDtypeStruct((t, BRANCH_WIDTH), BF16),
        grid=(batch, nq),
        in_specs=[
            pl.BlockSpec(blk, cur(0)),
            pl.BlockSpec(blk, prev(1)),
            pl.BlockSpec(blk, cur(1)),
            pl.BlockSpec(blk, prev(2)),
            pl.BlockSpec(blk, cur(2)),
            pl.BlockSpec((1, LANES), lambda b, n: (0, 0)),
        ],
        out_specs=pl.BlockSpec(blk, cur(0)),
        compiler_params=_params("arbitrary", "arbitrary"),
        name="swa_attn",
    )(qkv, qkv, qkv, qkv, qkv, sinks_row)


def _lin_kernel(f_ref, v_ref, g_ref, lg_ref, o_ref, st_sc, *, dk, gated, qoff, koff, laoff, roff, norm):
    width = BRANCH_HEADS * dk
    grp = LIN_GROUP
    ch = LINEAR_CHUNK
    seq = f_ref.shape[0]

    r_io = lax.broadcasted_iota(I32, (grp, grp), 0)
    c_io = lax.broadcasted_iota(I32, (grp, grp), 1)
    same = (r_io // ch) == (c_io // ch)
    ltri = (same & (c_io <= r_io)).astype(F32)
    lfull = same.astype(F32)
    pos_in_chunk = (lax.broadcasted_iota(I32, (grp, width), 0) % ch + 1).astype(F32)
    head_avg = ((r_io // HEAD_DIM) == (c_io // HEAD_DIM)).astype(F32) * (1.0 / HEAD_DIM)
    lane_k = lax.broadcasted_iota(I32, (1, width), 1) // dk
    lane_v = lax.broadcasted_iota(I32, (1, BRANCH_WIDTH), 1) // HEAD_DIM
    bd_mask = (lax.broadcasted_iota(I32, (BRANCH_WIDTH, width), 0) // HEAD_DIM
               == lax.broadcasted_iota(I32, (BRANCH_WIDTH, width), 1) // dk)
    gain = g_ref[...]

    st_sc[...] = jnp.zeros(st_sc.shape, F32)

    def group(gi, carry):
        r0 = pl.multiple_of(gi * grp, grp)
        rows = pl.ds(r0, grp)
        q = f_ref[rows, qoff:qoff + width]
        k = f_ref[rows, koff:koff + width]
        if gated:
            la = f_ref[rows, laoff:laoff + width]
            b = _dot(ltri, la, HI)
            be = _dot(lfull, la, HI)
        else:
            b = pos_in_chunk * lg_ref[...]
            be = float(ch) * lg_ref[...]
        qd = (q * jnp.exp(b)).astype(BF16)
        kin = (k * jnp.exp(-b)).astype(BF16)
        kend = (k * jnp.exp(be - b)).astype(BF16)
        dec = jnp.exp(be)
        dec_rows = [dec[c * ch:c * ch + 1, :] if gated else dec for c in range(grp // ch)]
        v = v_ref[rows, :]
        zq = jnp.zeros_like(qd)
        zv = jnp.zeros_like(v)

        o = jnp.zeros((grp, BRANCH_WIDTH), F32)
        for h in range(BRANCH_HEADS):
            a = _dot_nt(jnp.where(lane_k == h, qd, zq), kin) * ltri
            o = o + _dot(a.astype(BF16), jnp.where(lane_v == h, v, zv))

        st = st_sc[...]
        parts = []
        for c in range(grp // ch):
            sl = slice(c * ch, (c + 1) * ch)
            parts.append(_dot_nt(qd[sl], st.astype(BF16)))
            kv_t = _dot_tn(v[sl], kend[sl])
            st = st * dec_rows[c] + jnp.where(bd_mask, kv_t, 0.0)
        st_sc[...] = st
        o = o + jnp.concatenate(parts, axis=0)

        if norm == "rms":
            ms = _dot(o * o, head_avg, HI)
            y = o * lax.rsqrt(ms + NORM_EPS) * gain
        else:
            mu = _dot(o, head_avg, HI)
            xc = o - mu
            var = _dot(xc * xc, head_avg, HI)
            y = xc * lax.rsqrt(var + NORM_EPS) * gain
        r = f_ref[rows, roff:roff + BRANCH_WIDTH]
        o_ref[rows, :] = (y * jax.nn.silu(r)).astype(BF16)
        return carry

    lax.fori_loop(0, seq // grp, group, 0)


def _linear_attention(feat, v, gain, log_gamma_row, batch, seq, *, dk, gated, qoff, koff, laoff, roff, norm, name):
    t, fw = feat.shape
    width = BRANCH_HEADS * dk
    kern = functools.partial(_lin_kernel, dk=dk, gated=gated, qoff=qoff, koff=koff, laoff=laoff, roff=roff, norm=norm)
    return pl.pallas_call(
        kern,
        out_shape=jax.ShapeDtypeStruct((t, BRANCH_WIDTH), BF16),
        grid=(batch,),
        in_specs=[
            pl.BlockSpec((seq, fw), lambda b: (b, 0)),
            pl.BlockSpec((seq, BRANCH_WIDTH), lambda b: (b, 0)),
            pl.BlockSpec((1, BRANCH_WIDTH), lambda b: (0, 0)),
            pl.BlockSpec((1, width), lambda b: (0, 0)),
        ],
        out_specs=pl.BlockSpec((seq, BRANCH_WIDTH), lambda b: (b, 0)),
        scratch_shapes=[pltpu.VMEM((BRANCH_WIDTH, width), F32)],
        compiler_params=_params("arbitrary"),
        name=name,
    )(feat, v, gain, log_gamma_row)


def _merge_kernel(x_ref, sc1_ref, sh1_ref, gt1_ref, gnm_ref, ym_ref, yg_ref, yr_ref, ys_ref,
                  wm_ref, wb_ref, wo_ref, gnf_ref, sc2_ref, sh2_ref, wr_ref, br_ref,
                  x1_ref, h2_ref, ti_ref, tw_ref):
    x = x_ref[...]
    d = x.shape[1]
    hb = _norm_mod(x, gnm_ref[...], sc1_ref[...], sh1_ref[...]).astype(BF16)
    mixed = jnp.zeros(x.shape, F32)
    for n, y_ref in enumerate((ym_ref, yg_ref, yr_ref, ys_ref)):
        gate = jax.nn.sigmoid(_dot(hb, wm_ref[:, n * d:(n + 1) * d]))
        mixed = mixed + gate * _dot(y_ref[...], wb_ref[n])
    x1 = x + gt1_ref[...] * _dot(mixed.astype(BF16), wo_ref[...])
    x1_ref[...] = x1

    h2 = _norm_mod(x1, gnf_ref[...], sc2_ref[...], sh2_ref[...])
    sub = d // LANES
    for c in range(sub):
        h2_ref[pl.ds(c, x.shape[0], stride=sub), :] = h2[:, c * LANES:(c + 1) * LANES]
    logits = _dot(h2, wr_ref[...], HI) + br_ref[...]
    lane = lax.broadcasted_iota(I32, logits.shape, 1)
    vals, idxs = [], []
    cur = logits
    for _ in range(TOP_K):
        m = jnp.max(cur, axis=1, keepdims=True)
        idx = jnp.min(jnp.where(cur == m, lane, LANES), axis=1, keepdims=True)
        vals.append(m)
        idxs.append(idx)
        cur = jnp.where(lane == idx, -jnp.inf, cur)
    es = [jnp.exp(v - vals[0]) for v in vals]
    tot = es[0] + es[1] + es[2] + es[3]
    ti = jnp.zeros(logits.shape, I32)
    tw = jnp.zeros(logits.shape, F32)
    for k in range(TOP_K):
        ti = jnp.where(lane == k, idxs[k], ti)
        tw = jnp.where(lane == k, es[k] / tot, tw)
    ti_ref[...] = ti
    tw_ref[...] = tw


def _merge(x2, sc1, sh1, gt1, gnm, ys, wm, wb, wo, gnf, sc2, sh2, wr, br, seq):
    t, d = x2.shape
    tiles_per_seq = seq // TM_MERGE
    row = lambda i: (i, 0)
    per_batch = lambda i: (i // tiles_per_seq, 0, 0)
    c2 = lambda i: (0, 0)
    c3 = lambda i: (0, 0, 0)
    vec = pl.BlockSpec((None, 1, d), per_batch)
    ytile = pl.BlockSpec((TM_MERGE, BRANCH_WIDTH), row)
    return pl.pallas_call(
        _merge_kernel,
        out_shape=(
            jax.ShapeDtypeStruct((t, d), F32),
            jax.ShapeDtypeStruct((t * (d // LANES), LANES), F32),
            jax.ShapeDtypeStruct((t, LANES), I32),
            jax.ShapeDtypeStruct((t, LANES), F32),
        ),
        grid=(t // TM_MERGE,),
        in_specs=[
            pl.BlockSpec((TM_MERGE, d), row), vec, vec, vec,
            pl.BlockSpec((1, d), c2),
            ytile, ytile, ytile, ytile,
            pl.BlockSpec((d, N_BRANCHES * d), c2),
            pl.BlockSpec((N_BRANCHES, BRANCH_WIDTH, d), c3),
            pl.BlockSpec((d, d), c2),
            pl.BlockSpec((1, d), c2), vec, vec,
            pl.BlockSpec((d, LANES), c2),
            pl.BlockSpec((1, LANES), c2),
        ],
        out_specs=(
            pl.BlockSpec((TM_MERGE, d), row),
            pl.BlockSpec((TM_MERGE * (d // LANES), LANES), row),
            pl.BlockSpec((TM_MERGE, LANES), row),
            pl.BlockSpec((TM_MERGE, LANES), row),
        ),
        compiler_params=_params("arbitrary"),
        name="merge_router",
    )(x2, sc1, sh1, gt1, gnm, *ys, wm, wb, wo, gnf, sc2, sh2, wr, br)


def _route(top_i, top_w, n_tok, n_tiles):
    tm = TM_EXPERT
    flat_e = top_i.reshape(-1)
    flat_w = top_w.reshape(-1)
    n_assign = flat_e.shape[0]
    sorted_e, order, sorted_w = lax.sort((flat_e, jnp.arange(n_assign, dtype=I32), flat_w), num_keys=1, is_stable=True)
    counts = jnp.sum((flat_e[:, None] == jnp.arange(N_EXPERTS, dtype=I32)[None, :]).astype(I32), axis=0)
    bounds = jnp.concatenate([jnp.zeros((1,), I32), jnp.cumsum(counts).astype(I32)])
    padded = ((counts + tm - 1) // tm) * tm
    gend = jnp.cumsum(padded)
    gstart = gend - padded
    tile_start = jnp.arange(n_tiles, dtype=I32) * tm
    tile_e_raw = jnp.sum((tile_start[:, None] >= gend[None, :]).astype(I32), axis=1)
    tile_e = jnp.minimum(tile_e_raw, N_EXPERTS - 1).astype(I32)
    n_used = (gend[-1] // tm).astype(I32).reshape(1)
    t_off = tile_start - gstart[tile_e]
    t_cnt = jnp.where(tile_e_raw < N_EXPERTS, counts[tile_e], 0)
    row = jnp.arange(tm, dtype=I32)[None, :]
    j = t_off[:, None] + row
    valid = j < t_cnt[:, None]
    p = jnp.clip(bounds[tile_e][:, None] + j, 0, n_assign - 1)
    a = order[p]
    tok = a // TOP_K
    kk = a % TOP_K
    src = jnp.where(valid, tok, 0).astype(I32)
    dst = jnp.where(valid, kk * n_tok + tok, TOP_K * n_tok + row).astype(I32)
    w = jnp.where(valid, sorted_w[p], 0.0).astype(F32)
    tile_nv = jnp.clip(t_cnt - t_off, 0, tm).astype(I32)
    return tile_e, n_used, tile_nv, src.reshape(-1), dst.reshape(-1), w.reshape(-1, 1)


def _moe_kernel(te_ref, nu_ref, nv_ref, src_hbm, dst_hbm, w_ref, h_hbm, wgu_ref, bgu_ref, wd_ref, bd_ref,
                y_hbm, src_s, dst_s, xs, ys, wgu_b, wd_b, isem, gsem, ssem):
    i = pl.program_id(0)
    tm = TM_EXPERT
    sub = xs.shape[1] // tm
    d = sub * LANES
    n_used = nu_ref[0]

    def tok_rows(t):
        return pl.ds(pl.multiple_of(t * sub, sub), sub)

    def idx_copies(tile, buf):
        base = pl.multiple_of(tile * tm, tm)
        return (pltpu.make_async_copy(src_hbm.at[pl.ds(base, tm)], src_s.at[buf], isem.at[buf, 0]),
                pltpu.make_async_copy(dst_hbm.at[pl.ds(base, tm)], dst_s.at[buf], isem.at[buf, 1]))

    def start_idx(tile, buf):
        for cp in idx_copies(tile, buf):
            cp.start()

    def wait_idx(tile, buf):
        for cp in idx_copies(tile, buf):
            cp.wait()

    def groups(tile):
        return lax.shift_right_logical(nv_ref[tile] + (ROW_GROUP - 1), ROW_GROUP.bit_length() - 1)

    def moved_rows(tile):
        return pl.multiple_of(groups(tile) * (ROW_GROUP * sub), ROW_GROUP * sub)

    def issue_gather(tile, buf, slot):
        def body(g, carry):
            for u in range(ROW_GROUP):
                r = g * ROW_GROUP + u
                pltpu.make_async_copy(h_hbm.at[tok_rows(src_s[buf, r])], xs.at[slot, tok_rows(r)], gsem.at[slot]).start()
            return carry
        lax.fori_loop(0, groups(tile), body, 0)

    def wait_gather(tile, slot):
        n = moved_rows(tile)
        pltpu.make_async_copy(h_hbm.at[pl.ds(0, n)], xs.at[slot, pl.ds(0, n)], gsem.at[slot]).wait()

    def wait_scatter(tile):
        n = moved_rows(tile)
        pltpu.make_async_copy(ys.at[pl.ds(0, n)], y_hbm.at[pl.ds(0, n)], ssem.at[0]).wait()

    @pl.when(i == 0)
    def _():
        ys[...] = jnp.zeros(ys.shape, F32)
        fill = pltpu.make_async_copy(ys, y_hbm.at[pl.ds(y_hbm.shape[0] - tm * sub, tm * sub)], ssem.at[0])
        fill.start()
        fill.wait()
        xs[...] = jnp.zeros(xs.shape, F32)

        @pl.when(n_used > 0)
        def _():
            start_idx(0, 0)
            wait_idx(0, 0)
            issue_gather(0, 0, 0)

        @pl.when(n_used > 1)
        def _():
            start_idx(1, 1)

    @pl.when(i < n_used)
    def _():
        slot = lax.rem(i, 2)
        buf = lax.rem(i, 3)

        @pl.when(i + 1 < n_used)
        def _():
            nbuf = lax.rem(i + 1, 3)
            wait_idx(i + 1, nbuf)
            issue_gather(i + 1, nbuf, 1 - slot)

        @pl.when(i + 2 < n_used)
        def _():
            start_idx(i + 2, lax.rem(i + 2, 3))

        @pl.when((i == 0) | (te_ref[i] != te_ref[jnp.maximum(i - 1, 0)]))
        def _():
            rows = d // 8

            def cast(c, carry):
                r = pl.ds(pl.multiple_of(c * rows, rows), rows)
                wgu_b[r, :] = wgu_ref[r, :].astype(BF16)
                wd_b[r, :] = wd_ref[r, :].astype(BF16)
                return carry
            lax.fori_loop(0, 8, cast, 0)

        wait_gather(i, slot)
        x = jnp.concatenate([xs[slot, pl.ds(c, tm, stride=sub), :] for c in range(sub)], axis=1).astype(BF16)
        gu = _dot(x, wgu_b[...]) + bgu_ref[...]
        x_glu = jnp.minimum(gu[:, :d], SWIGLU_LIMIT)
        x_lin = jnp.clip(gu[:, d:], -SWIGLU_LIMIT, SWIGLU_LIMIT)
        act = x_glu * jax.nn.sigmoid(SWIGLU_ALPHA * x_glu) * (x_lin + 1.0)
        out = (_dot(act.astype(BF16), wd_b[...]) + bd_ref[...]) * w_ref[...]

        @pl.when(i > 0)
        def _():
            wait_scatter(i - 1)

        for c in range(sub):
            ys[pl.ds(c, tm, stride=sub), :] = out[:, c * LANES:(c + 1) * LANES]

        def scatter(g, carry):
            for u in range(ROW_GROUP):
                r = g * ROW_GROUP + u
                pltpu.make_async_copy(ys.at[tok_rows(r)], y_hbm.at[tok_rows(dst_s[buf, r])], ssem.at[0]).start()
            return carry
        lax.fori_loop(0, groups(i), scatter, 0)

        @pl.when(i == n_used - 1)
        def _():
            wait_scatter(i)


def _moe(h2_rows, tile_e, n_used, tile_nv, src, dst, w_slot, wgu, bgu, wd, bd, layer):
    d = wgu.shape[2]
    sub = d // LANES
    t = h2_rows.shape[0] // sub
    n_exp = wgu.shape[1]
    n_tiles = tile_e.shape[0]
    tm = TM_EXPERT
    any_spec = pl.BlockSpec(memory_space=pl.ANY)
    expert = lambda i, te, nu, nv: (layer, te[i], 0, 0)
    grid_spec = pltpu.PrefetchScalarGridSpec(
        num_scalar_prefetch=3,
        grid=(n_tiles,),
        in_specs=[
            any_spec, any_spec,
            pl.BlockSpec((tm, 1), lambda i, te, nu, nv: (i, 0)),
            any_spec,
            pl.BlockSpec((None, None, d, 2 * d), expert),
            pl.BlockSpec((None, None, 1, 2 * d), expert),
            pl.BlockSpec((None, None, d, d), expert),
            pl.BlockSpec((None, None, 1, d), expert),
        ],
        out_specs=any_spec,
        scratch_shapes=[
            pltpu.SMEM((3, tm), I32),
            pltpu.SMEM((3, tm), I32),
            pltpu.VMEM((2, tm * sub, LANES), F32),
            pltpu.VMEM((tm * sub, LANES), F32),
            pltpu.VMEM((d, 2 * d), BF16),
            pltpu.VMEM((d, d), BF16),
            pltpu.SemaphoreType.DMA((3, 2)),
            pltpu.SemaphoreType.DMA((2,)),
            pltpu.SemaphoreType.DMA((1,)),
        ],
    )
    return pl.pallas_call(
        _moe_kernel,
        out_shape=jax.ShapeDtypeStruct(((TOP_K * t + tm) * sub, LANES), F32),
        grid_spec=grid_spec,
        compiler_params=pltpu.CompilerParams(dimension_semantics=("arbitrary",), has_side_effects=True),
        name="moe_grouped",
    )(tile_e, n_used, tile_nv, src, dst, w_slot, h2_rows, wgu, bgu.reshape(-1, n_exp, 1, 2 * d), wd, bd.reshape(-1, n_exp, 1, d))


def _combine_kernel(x_ref, gt_ref, y0_ref, y1_ref, y2_ref, y3_ref, gf_ref, o_ref, y_sc, *, final):
    tm, d = x_ref.shape
    sub = d // LANES
    y = (y0_ref[...] + y1_ref[...]) + (y2_ref[...] + y3_ref[...])
    y_sc[...] = y
    y = jnp.concatenate([y_sc[pl.ds(c, tm, stride=sub), :] for c in range(sub)], axis=1)
    x = x_ref[...] + gt_ref[...] * y
    if final:
        ms = jnp.mean(x * x, axis=-1, keepdims=True)
        x = x * lax.rsqrt(ms + NORM_EPS) * gf_ref[...]
    o_ref[...] = x


def _combine(x1, gate2, planes, g_final, seq, final):
    t, d = x1.shape
    tiles_per_seq = seq // TM_COMBINE
    nt = t // TM_COMBINE
    row = lambda i: (i, 0)
    plane = lambda k: (lambda i: (k * nt + i, 0))
    blk = (TM_COMBINE, d)
    pblk = (TM_COMBINE * (d // LANES), LANES)
    return pl.pallas_call(
        functools.partial(_combine_kernel, final=final),
        out_shape=jax.ShapeDtypeStruct((t, d), F32),
        grid=(nt,),
        in_specs=[
            pl.BlockSpec(blk, row),
            pl.BlockSpec((None, 1, d), lambda i: (i // tiles_per_seq, 0, 0)),
            pl.BlockSpec(pblk, plane(0)), pl.BlockSpec(pblk, plane(1)),
            pl.BlockSpec(pblk, plane(2)), pl.BlockSpec(pblk, plane(3)),
            pl.BlockSpec((1, d), lambda i: (0, 0)),
        ],
        out_specs=pl.BlockSpec(blk, row),
        scratch_shapes=[pltpu.VMEM(pblk, F32)],
        compiler_params=_params("arbitrary"),
        name="moe_combine",
    )(x1, gate2, planes, planes, planes, planes, g_final)


def _pack_w_in(w):
    sizes = (256, 256, 256, 128, 128, 256, GLA_GATE_RANK, 256, 256, 256, 256, 256, 256, 128, 128)
    offs = np.concatenate([[0], np.cumsum(sizes)])
    mq, mk, mv, gq, gk, gv, ga, gr, rq, rk, rv, rg, sq, sk, sv = (w[:, int(offs[n]):int(offs[n + 1])] for n in range(15))
    merge = w[:, int(offs[15]):]
    d = w.shape[0]
    rep = lambda kv: jnp.repeat(kv.reshape(d, SWA_KV_HEADS, HEAD_DIM), BRANCH_HEADS // SWA_KV_HEADS, axis=1).reshape(d, BRANCH_WIDTH)
    scale = HEAD_DIM ** -0.5
    ga_pad = jnp.pad(ga, ((0, 0), (0, LANES - GLA_GATE_RANK)))
    packed = jnp.concatenate([mq * scale, mk, mv, gq, gk, gv, ga_pad, gr, rq, rk * scale, rv, rg,
                              sq * scale, rep(sk), rep(sv)], axis=1)
    return packed.astype(BF16), merge.astype(BF16)


def kernel(x, c, w_ada, b_ada, g_norm_mix, w_in, w_gla_gate, b_gla_gate, g_gla_norm, g_ret_norm, attn_sinks,
           w_branch, w_out, g_norm_ffn, w_router, b_router, w_gate_up, b_gate_up, w_down, b_down, g_final):
    batch, seq, d = x.shape
    depth = w_ada.shape[0]
    t = batch * seq
    n_alibi = 2 * BRANCH_HEADS
    slopes = [2.0 ** (-(k + 1.0) * (8.0 / n_alibi)) for k in range(n_alibi)]
    swa_slopes, moba_slopes = tuple(slopes[:BRANCH_HEADS]), tuple(slopes[BRANCH_HEADS:])
    log_gamma = jnp.log(1.0 - 2.0 ** (-RET_DECAY_BASE - jnp.arange(BRANCH_HEADS, dtype=F32)))
    log_gamma_row = jnp.repeat(log_gamma, HEAD_DIM).reshape(1, BRANCH_WIDTH)
    dummy_row = jnp.zeros((1, BRANCH_HEADS * GLA_KEY_DIM), F32)
    n_tiles = (TOP_K * t + N_EXPERTS * (TM_EXPERT - 1) + TM_EXPERT - 1) // TM_EXPERT

    mod = _modulation(c, w_ada, b_ada)
    xf = x.reshape(t, d)
    for l in range(depth):
        shift1, scale1, gate1, shift2, scale2, gate2 = (
            mod[l, :, n * d:(n + 1) * d].reshape(batch, 1, d) for n in range(6))
        w_packed, w_merge = _pack_w_in(w_in[l])
        wg = jnp.pad(w_gla_gate[l], ((0, LANES - GLA_GATE_RANK), (0, 0)))
        moba_qkv, gla_f, gla_v, ret_f, ret_v, swa_qkv, kmean = _input_projection(
            xf, scale1, shift1, g_norm_mix[l].reshape(1, d), w_packed, wg, b_gla_gate[l].reshape(1, -1), seq)
        kmean = kmean.reshape(batch, seq // MOBA_BLOCK, BRANCH_WIDTH)

        y_moba = _moba(moba_qkv, kmean, batch, seq, moba_slopes)
        y_gla = _linear_attention(gla_f, gla_v, g_gla_norm[l].reshape(1, -1), dummy_row, batch, seq,
                                  dk=GLA_KEY_DIM, gated=True, qoff=0, koff=128, laoff=256, roff=384,
                                  norm="rms", name="gla")
        y_ret = _linear_attention(ret_f, ret_v, g_ret_norm[l].reshape(1, -1), log_gamma_row, batch, seq,
                                  dk=HEAD_DIM, gated=False, qoff=0, koff=256, laoff=0, roff=512,
                                  norm="group", name="retention")
        sinks_row = jnp.pad(attn_sinks[l].reshape(1, -1), ((0, 0), (0, LANES - BRANCH_HEADS)))
        y_swa = _swa(swa_qkv, sinks_row, batch, seq, swa_slopes)

        wr = jnp.pad(w_router[l], ((0, 0), (0, LANES - N_EXPERTS)))
        br = jnp.pad(b_router[l].reshape(1, -1), ((0, 0), (0, LANES - N_EXPERTS)), constant_values=NEG_BIG)
        x1, h2, top_i, top_w = _merge(
            xf, scale1, shift1, gate1, g_norm_mix[l].reshape(1, d), (y_moba, y_gla, y_ret, y_swa),
            w_merge, w_branch[l].astype(BF16), w_out[l].astype(BF16),
            g_norm_ffn[l].reshape(1, d), scale2, shift2, wr, br, seq)

        tile_e, n_used, tile_nv, src, dst, w_slot = _route(top_i[:, :TOP_K], top_w[:, :TOP_K], t, n_tiles)
        planes = _moe(h2, tile_e, n_used, tile_nv, src, dst, w_slot, w_gate_up, b_gate_up, w_down, b_down, l)
        xf = _combine(x1, gate2, planes, g_final.reshape(1, d), seq, final=(l == depth - 1))
    return xf.reshape(batch, seq, d)
```

```python
import functools

import jax
import jax.numpy as jnp
import numpy as np
from jax import lax
from jax.experimental import pallas as pl
from jax.experimental.pallas import tpu as pltpu

F32 = jnp.float32
BF16 = jnp.bfloat16
I32 = jnp.int32
HI = lax.Precision.HIGHEST

HEAD_DIM = 64
N_BRANCHES = 4
BRANCH_HEADS = 4
BRANCH_WIDTH = HEAD_DIM * BRANCH_HEADS
MOBA_BLOCK = 256
MOBA_TOPK = 3
GLA_KEY_DIM = 32
GLA_GATE_RANK = 16
GLA_GATE_TEMP = 16.0
LINEAR_CHUNK = 64
RET_DECAY_BASE = 5.0
SWA_KV_HEADS = 2
WINDOW = 128
N_EXPERTS = 32
TOP_K = 4
SWIGLU_ALPHA = 1.702
SWIGLU_LIMIT = 7.0
NORM_EPS = 1e-5

LANES = 128
TM_IN = 512
TM_MERGE = 512
TM_EXPERT = 256
ROW_GROUP = 8
TM_COMBINE = 512
LIN_GROUP = 256
MOD_COLS = 1536
NEG_BIG = -1e30

_C_MOBA = (0, 768)
_C_GLA = (768, 1664)
_C_RET = (1664, 2688)
_C_SWA = (2688, 3456)
N_PACKED = 3456


def _dot(a, b, precision=None):
    return jnp.dot(a, b, preferred_element_type=F32, precision=precision)


def _dot_nt(a, b, precision=None):
    return lax.dot_general(a, b, (((1,), (1,)), ((), ())), preferred_element_type=F32, precision=precision)


def _dot_tn(a, b, precision=None):
    return lax.dot_general(a, b, (((0,), (0,)), ((), ())), preferred_element_type=F32, precision=precision)


def _split2(x):
    hi = x.astype(BF16)
    return hi, (x - hi.astype(F32)).astype(BF16)


def _dot_exact_lhs(m, x):
    hi, lo = _split2(x)
    mb = m.astype(BF16)
    return _dot(mb, hi) + _dot(mb, lo)


def _dot_exact_rhs(x, m):
    hi, lo = _split2(x)
    mb = m.astype(BF16)
    return _dot(hi, mb) + _dot(lo, mb)


def _norm_mod(x, g, scale, shift):
    ms = jnp.mean(x * x, axis=-1, keepdims=True)
    y = x * lax.rsqrt(ms + NORM_EPS) * g
    return y * (1.0 + scale) + shift


def _log_sigmoid(x):
    return jnp.minimum(x, 0.0) - jnp.log1p(jnp.exp(-jnp.abs(x)))


def _params(*sem):
    return pltpu.CompilerParams(dimension_semantics=tuple(sem))


def _mod_kernel(c_ref, w_ref, b_ref, o_ref):
    c = c_ref[...]
    o_ref[...] = _dot(jax.nn.silu(c), w_ref[...], HI) + b_ref[...]


def _modulation(c, w_ada, b_ada):
    depth, d, n = w_ada.shape
    b = c.shape[0]
    return pl.pallas_call(
        _mod_kernel,
        out_shape=jax.ShapeDtypeStruct((depth, b, n), F32),
        grid=(depth, n // MOD_COLS),
        in_specs=[
            pl.BlockSpec((b, d), lambda l, j: (0, 0)),
            pl.BlockSpec((None, d, MOD_COLS), lambda l, j: (l, 0, j)),
            pl.BlockSpec((None, 1, MOD_COLS), lambda l, j: (l, 0, j)),
        ],
        out_specs=pl.BlockSpec((None, b, MOD_COLS), lambda l, j: (l, 0, j)),
        compiler_params=_params("arbitrary", "arbitrary"),
        name="adaln_mod",
    )(c, w_ada, b_ada.reshape(depth, 1, n))


def _in_kernel(x_ref, sc_ref, sh_ref, g_ref, w_ref, wg_ref, bg_ref,
               moba_ref, glaf_ref, glav_ref, retf_ref, retv_ref, swa_ref, kmean_ref):
    h = _norm_mod(x_ref[...], g_ref[...], sc_ref[...], sh_ref[...]).astype(BF16)

    zm = _dot(h, w_ref[:, _C_MOBA[0]:_C_MOBA[1]])
    moba_ref[...] = zm.astype(BF16)
    for blk in range(TM_IN // MOBA_BLOCK):
        kb = zm[blk * MOBA_BLOCK:(blk + 1) * MOBA_BLOCK, 256:512]
        kmean_ref[blk:blk + 1, :] = jnp.mean(kb, axis=0, keepdims=True)

    zg = _dot(h, w_ref[:, _C_GLA[0]:_C_GLA[1]])
    gate_logit = _dot(zg[:, 512:640], wg_ref[...], HI) + bg_ref[...]
    glaf_ref[:, 0:128] = zg[:, 0:128] * (GLA_KEY_DIM ** -0.5)
    glaf_ref[:, 128:256] = zg[:, 128:256]
    glaf_ref[:, 256:384] = _log_sigmoid(gate_logit) / GLA_GATE_TEMP
    glaf_ref[:, 384:640] = zg[:, 640:896]
    glav_ref[...] = zg[:, 256:512].astype(BF16)

    zr = _dot(h, w_ref[:, _C_RET[0]:_C_RET[1]])
    retf_ref[:, 0:512] = zr[:, 0:512]
    retf_ref[:, 512:768] = zr[:, 768:1024]
    retv_ref[...] = zr[:, 512:768].astype(BF16)

    swa_ref[...] = _dot(h, w_ref[:, _C_SWA[0]:_C_SWA[1]]).astype(BF16)


def _input_projection(x2, scale, shift, g, w_packed, wg, bg, seq):
    t, d = x2.shape
    tiles_per_seq = seq // TM_IN
    nblk = TM_IN // MOBA_BLOCK
    row = lambda i: (i, 0)
    per_batch = lambda i: (i // tiles_per_seq, 0, 0)
    const2 = lambda i: (0, 0)
    outs = pl.pallas_call(
        _in_kernel,
        out_shape=(
            jax.ShapeDtypeStruct((t, 768), BF16),
            jax.ShapeDtypeStruct((t, 640), F32),
            jax.ShapeDtypeStruct((t, 256), BF16),
            jax.ShapeDtypeStruct((t, 768), F32),
            jax.ShapeDtypeStruct((t, 256), BF16),
            jax.ShapeDtypeStruct((t, 768), BF16),
            jax.ShapeDtypeStruct((t // TM_IN, nblk, 256), F32),
        ),
        grid=(t // TM_IN,),
        in_specs=[
            pl.BlockSpec((TM_IN, d), row),
            pl.BlockSpec((None, 1, d), per_batch),
            pl.BlockSpec((None, 1, d), per_batch),
            pl.BlockSpec((1, d), const2),
            pl.BlockSpec((d, N_PACKED), const2),
            pl.BlockSpec((LANES, LANES), const2),
            pl.BlockSpec((1, LANES), const2),
        ],
        out_specs=(
            pl.BlockSpec((TM_IN, 768), row),
            pl.BlockSpec((TM_IN, 640), row),
            pl.BlockSpec((TM_IN, 256), row),
            pl.BlockSpec((TM_IN, 768), row),
            pl.BlockSpec((TM_IN, 256), row),
            pl.BlockSpec((TM_IN, 768), row),
            pl.BlockSpec((None, nblk, 256), lambda i: (i, 0, 0)),
        ),
        compiler_params=_params("arbitrary"),
        name="input_proj",
    )(x2, scale, shift, g, w_packed, wg, bg)
    return outs


def _moba_kernel(q_ref, k_ref, v_ref, km_ref, o_ref, sel_sc, m_sc, l_sc, acc_sc, *, slopes):
    i = pl.program_id(1)
    nb = MOBA_BLOCK
    n_blocks = km_ref.shape[0]
    q = q_ref[...]
    lane = lax.broadcasted_iota(I32, (1, BRANCH_WIDTH), 1)
    head_of_lane = lane // HEAD_DIM

    rows = lax.broadcasted_iota(I32, (LANES, BRANCH_WIDTH), 0)
    lanes_full = lax.broadcasted_iota(I32, (LANES, BRANCH_WIDTH), 1)
    km = km_ref[...]
    km_t = jnp.concatenate([km] * (LANES // n_blocks), axis=0)
    km_heads = jnp.where((rows // n_blocks) == (lanes_full // HEAD_DIM), km_t, 0.0)
    gate_t = _dot_nt(km_heads, q.astype(F32), HI)

    jrow = lax.broadcasted_iota(I32, (n_blocks, nb), 0)
    valid = jrow < i
    for h in range(BRANCH_HEADS):
        g = jnp.where(valid, gate_t[h * n_blocks:(h + 1) * n_blocks, :], -jnp.inf)
        cnt = jnp.zeros((n_blocks, nb), F32)
        for jp in range(n_blocks):
            gj = g[jp:jp + 1, :]
            beats = (gj > g) | ((gj == g) & (jp < jrow))
            cnt = cnt + beats.astype(F32)
        sel_sc[h * n_blocks:(h + 1) * n_blocks, :] = ((cnt < MOBA_TOPK) & valid).astype(F32)

    key_io = lax.broadcasted_iota(I32, (nb, nb), 0)
    qry_io = lax.broadcasted_iota(I32, (nb, nb), 1)
    rel = (qry_io - key_io).astype(F32)

    m_sc[...] = jnp.full(m_sc.shape, -jnp.inf, F32)
    l_sc[...] = jnp.zeros(l_sc.shape, F32)
    acc_sc[...] = jnp.zeros(acc_sc.shape, F32)

    def process(j, own):
        start = pl.multiple_of(j * nb, nb)
        kj = k_ref[pl.ds(start, nb), :]
        vj_t = v_ref[pl.ds(start, nb), :].astype(F32).T
        dist = rel + ((i - j) * nb).astype(F32)
        for h in range(BRANCH_HEADS):
            hm = head_of_lane == h
            s = _dot_nt(kj, jnp.where(hm, q, jnp.zeros_like(q))) - slopes[h] * dist
            if own:
                mask = rel >= 0.0
            else:
                mask = sel_sc[pl.ds(h * n_blocks + j, 1), :] > 0.5
            s = jnp.where(mask, s, -jnp.inf)
            m_old = m_sc[h]
            m_new = jnp.maximum(m_old, jnp.max(s, axis=0, keepdims=True))
            alpha = jnp.exp(m_old - m_new)
            p = jnp.exp(s - m_new)
            l_sc[h] = alpha * l_sc[h] + jnp.sum(p, axis=0, keepdims=True)
            m_sc[h] = m_new
            hrows = slice(h * HEAD_DIM, (h + 1) * HEAD_DIM)
            pv = _dot(vj_t[hrows, :].astype(BF16), p.astype(BF16))
            acc_sc[hrows, :] = acc_sc[hrows, :] * alpha + pv

    process(i, True)

    def body(j, carry):
        process(j, False)
        return carry

    lax.fori_loop(0, i, body, 0)

    for h in range(BRANCH_HEADS):
        hrows = slice(h * HEAD_DIM, (h + 1) * HEAD_DIM)
        acc_sc[hrows, :] = acc_sc[hrows, :] * (1.0 / l_sc[h])
    o_ref[...] = acc_sc[...].T.astype(BF16)


def _moba(qkv, kmean, batch, seq, slopes):
    t = qkv.shape[0]
    nq = seq // MOBA_BLOCK
    kern = functools.partial(_moba_kernel, slopes=slopes)
    return pl.pallas_call(
        kern,
        out_shape=jax.ShapeDtypeStruct((t, BRANCH_WIDTH), BF16),
        grid=(batch, nq),
        in_specs=[
            pl.BlockSpec((MOBA_BLOCK, BRANCH_WIDTH), lambda b, i: (b * nq + i, 0)),
            pl.BlockSpec((seq, BRANCH_WIDTH), lambda b, i: (b, 1)),
            pl.BlockSpec((seq, BRANCH_WIDTH), lambda b, i: (b, 2)),
            pl.BlockSpec((None, nq, BRANCH_WIDTH), lambda b, i: (b, 0, 0)),
        ],
        out_specs=pl.BlockSpec((MOBA_BLOCK, BRANCH_WIDTH), lambda b, i: (b * nq + i, 0)),
        scratch_shapes=[
            pltpu.VMEM((BRANCH_HEADS * nq, MOBA_BLOCK), F32),
            pltpu.VMEM((BRANCH_HEADS, 1, MOBA_BLOCK), F32),
            pltpu.VMEM((BRANCH_HEADS, 1, MOBA_BLOCK), F32),
            pltpu.VMEM((BRANCH_WIDTH, MOBA_BLOCK), F32),
        ],
        compiler_params=_params("arbitrary", "arbitrary"),
        name="moba_attn",
    )(qkv, qkv, qkv, kmean)


def _swa_kernel(q_ref, kp_ref, kc_ref, vp_ref, vc_ref, sink_ref, o_ref, *, slopes):
    n = pl.program_id(1)
    w = WINDOW
    q = q_ref[...]
    kp, kc, vp, vc = kp_ref[...], kc_ref[...], vp_ref[...], vc_ref[...]
    head_of_lane = lax.broadcasted_iota(I32, (1, BRANCH_WIDTH), 1) // HEAD_DIM
    key_io = lax.broadcasted_iota(I32, (w, w), 0)
    qry_io = lax.broadcasted_iota(I32, (w, w), 1)
    rel = (qry_io - key_io).astype(F32)
    allow_cur = rel >= 0.0
    allow_prev = (rel < 0.0) & (n > 0)
    zq = jnp.zeros_like(q)
    vp_t = vp.astype(F32).T
    vc_t = vc.astype(F32).T
    outs = []
    for h in range(BRANCH_HEADS):
        qh = jnp.where(head_of_lane == h, q, zq)
        sp = jnp.where(allow_prev, _dot_nt(kp, qh) - slopes[h] * (rel + float(w)), -jnp.inf)
        sc = jnp.where(allow_cur, _dot_nt(kc, qh) - slopes[h] * rel, -jnp.inf)
        sink = sink_ref[0:1, h:h + 1]
        m = jnp.maximum(jnp.maximum(jnp.max(sp, axis=0, keepdims=True), jnp.max(sc, axis=0, keepdims=True)), sink)
        pp = jnp.exp(sp - m)
        pc = jnp.exp(sc - m)
        l = jnp.sum(pp, axis=0, keepdims=True) + jnp.sum(pc, axis=0, keepdims=True) + jnp.exp(sink - m)
        hrows = slice(h * HEAD_DIM, (h + 1) * HEAD_DIM)
        o = _dot(vp_t[hrows, :].astype(BF16), pp.astype(BF16)) + _dot(vc_t[hrows, :].astype(BF16), pc.astype(BF16))
        outs.append(o * (1.0 / l))
    o_ref[...] = jnp.concatenate(outs, axis=0).T.astype(BF16)


def _swa(qkv, sinks_row, batch, seq, slopes):
    t = qkv.shape[0]
    nq = seq // WINDOW
    kern = functools.partial(_swa_kernel, slopes=slopes)
    cur = lambda col: (lambda b, n: (b * nq + n, col))
    prev = lambda col: (lambda b, n: (b * nq + jnp.maximum(n - 1, 0), col))
    blk = (WINDOW, BRANCH_WIDTH)
    return pl.pallas_call(
        kern,
        out_shape=jax.ShapeDtypeStruct((t, BRANCH_WIDTH), BF16),
        grid=(batch, nq),
        in_specs=[
            pl.BlockSpec(blk, cur(0)),
            pl.BlockSpec(blk, prev(1)),
            pl.BlockSpec(blk, cur(1)),
            pl.BlockSpec(blk, prev(2)),
            pl.BlockSpec(blk, cur(2)),
            pl.BlockSpec((1, LANES), lambda b, n: (0, 0)),
        ],
        out_specs=pl.BlockSpec(blk, cur(0)),
        compiler_params=_params("arbitrary", "arbitrary"),
        name="swa_attn",
    )(qkv, qkv, qkv, qkv, qkv, sinks_row)


def _lin_kernel(f_ref, v_ref, g_ref, lg_ref, o_ref, st_sc, *, dk, gated, qoff, koff, laoff, roff, norm):
    width = BRANCH_HEADS * dk
    grp = LIN_GROUP
    ch = LINEAR_CHUNK
    seq = f_ref.shape[0]

    r_io = lax.broadcasted_iota(I32, (grp, grp), 0)
    c_io = lax.broadcasted_iota(I32, (grp, grp), 1)
    same = (r_io // ch) == (c_io // ch)
    ltri = (same & (c_io <= r_io)).astype(F32)
    lfull = same.astype(F32)
    pos_in_chunk = (lax.broadcasted_iota(I32, (grp, width), 0) % ch + 1).astype(F32)
    head_avg = ((r_io // HEAD_DIM) == (c_io // HEAD_DIM)).astype(F32) * (1.0 / HEAD_DIM)
    lane_k = lax.broadcasted_iota(I32, (1, width), 1) // dk
    lane_v = lax.broadcasted_iota(I32, (1, BRANCH_WIDTH), 1) // HEAD_DIM
    bd_mask = (lax.broadcasted_iota(I32, (BRANCH_WIDTH, width), 0) // HEAD_DIM
               == lax.broadcasted_iota(I32, (BRANCH_WIDTH, width), 1) // dk)
    gain = g_ref[...]

    st_sc[...] = jnp.zeros(st_sc.shape, F32)

    def group(gi, carry):
        r0 = pl.multiple_of(gi * grp, grp)
        rows = pl.ds(r0, grp)
        q = f_ref[rows, qoff:qoff + width]
        k = f_ref[rows, koff:koff + width]
        if gated:
            la = f_ref[rows, laoff:laoff + width]
            b = _dot_exact_lhs(ltri, la)
            be = _dot_exact_lhs(lfull, la)
        else:
            b = pos_in_chunk * lg_ref[...]
            be = float(ch) * lg_ref[...]
        qd = (q * jnp.exp(b)).astype(BF16)
        kin = (k * jnp.exp(-b)).astype(BF16)
        kend = (k * jnp.exp(be - b)).astype(BF16)
        dec = jnp.exp(be)
        dec_rows = [dec[c * ch:c * ch + 1, :] if gated else dec for c in range(grp // ch)]
        v = v_ref[rows, :]
        zq = jnp.zeros_like(qd)
        zv = jnp.zeros_like(v)

        o = jnp.zeros((grp, BRANCH_WIDTH), F32)
        for h in range(BRANCH_HEADS):
            a = _dot_nt(jnp.where(lane_k == h, qd, zq), kin) * ltri
            o = o + _dot(a.astype(BF16), jnp.where(lane_v == h, v, zv))

        st = st_sc[...]
        parts = []
        for c in range(grp // ch):
            sl = slice(c * ch, (c + 1) * ch)
            parts.append(_dot_nt(qd[sl], st.astype(BF16)))
            kv_t = _dot_tn(v[sl], kend[sl])
            st = st * dec_rows[c] + jnp.where(bd_mask, kv_t, 0.0)
        st_sc[...] = st
        o = o + jnp.concatenate(parts, axis=0)

        if norm == "rms":
            ms = _dot_exact_rhs(o * o, head_avg)
            y = o * lax.rsqrt(ms + NORM_EPS) * gain
        else:
            mu = _dot_exact_rhs(o, head_avg)
            xc = o - mu
            var = _dot_exact_rhs(xc * xc, head_avg)
            y = xc * lax.rsqrt(var + NORM_EPS) * gain
        r = f_ref[rows, roff:roff + BRANCH_WIDTH]
        o_ref[rows, :] = (y * jax.nn.silu(r)).astype(BF16)
        return carry

    lax.fori_loop(0, seq // grp, group, 0)


def _linear_attention(feat, v, gain, log_gamma_row, batch, seq, *, dk, gated, qoff, koff, laoff, roff, norm, name):
    t, fw = feat.shape
    width = BRANCH_HEADS * dk
    kern = functools.partial(_lin_kernel, dk=dk, gated=gated, qoff=qoff, koff=koff, laoff=laoff, roff=roff, norm=norm)
    return pl.pallas_call(
        kern,
        out_shape=jax.ShapeDtypeStruct((t, BRANCH_WIDTH), BF16),
        grid=(batch,),
        in_specs=[
            pl.BlockSpec((seq, fw), lambda b: (b, 0)),
            pl.BlockSpec((seq, BRANCH_WIDTH), lambda b: (b, 0)),
            pl.BlockSpec((1, BRANCH_WIDTH), lambda b: (0, 0)),
            pl.BlockSpec((1, width), lambda b: (0, 0)),
        ],
        out_specs=pl.BlockSpec((seq, BRANCH_WIDTH), lambda b: (b, 0)),
        scratch_shapes=[pltpu.VMEM((BRANCH_WIDTH, width), F32)],
        compiler_params=_params("arbitrary"),
        name=name,
    )(feat, v, gain, log_gamma_row)


def _merge_kernel(x_ref, sc1_ref, sh1_ref, gt1_ref, gnm_ref, ym_ref, yg_ref, yr_ref, ys_ref,
                  wm_ref, wb_ref, wo_ref, gnf_ref, sc2_ref, sh2_ref, wr_ref, br_ref,
                  x1_ref, h2_ref, ti_ref, tw_ref):
    x = x_ref[...]
    d = x.shape[1]
    hb = _norm_mod(x, gnm_ref[...], sc1_ref[...], sh1_ref[...]).astype(BF16)
    mixed = jnp.zeros(x.shape, F32)
    for n, y_ref in enumerate((ym_ref, yg_ref, yr_ref, ys_ref)):
        gate = jax.nn.sigmoid(_dot(hb, wm_ref[:, n * d:(n + 1) * d]))
        mixed = mixed + gate * _dot(y_ref[...], wb_ref[n])
    x1 = x + gt1_ref[...] * _dot(mixed.astype(BF16), wo_ref[...])
    x1_ref[...] = x1

    h2 = _norm_mod(x1, gnf_ref[...], sc2_ref[...], sh2_ref[...])
    sub = d // LANES
    for c in range(sub):
        h2_ref[pl.ds(c, x.shape[0], stride=sub), :] = h2[:, c * LANES:(c + 1) * LANES]
    logits = _dot(h2, wr_ref[...], HI) + br_ref[...]
    lane = lax.broadcasted_iota(I32, logits.shape, 1)
    vals, idxs = [], []
    cur = logits
    for _ in range(TOP_K):
        m = jnp.max(cur, axis=1, keepdims=True)
        idx = jnp.min(jnp.where(cur == m, lane, LANES), axis=1, keepdims=True)
        vals.append(m)
        idxs.append(idx)
        cur = jnp.where(lane == idx, -jnp.inf, cur)
    es = [jnp.exp(v - vals[0]) for v in vals]
    tot = es[0] + es[1] + es[2] + es[3]
    ti = jnp.zeros(logits.shape, I32)
    tw = jnp.zeros(logits.shape, F32)
    for k in range(TOP_K):
        ti = jnp.where(lane == k, idxs[k], ti)
        tw = jnp.where(lane == k, es[k] / tot, tw)
    ti_ref[...] = ti
    tw_ref[...] = tw


def _merge(x2, sc1, sh1, gt1, gnm, ys, wm, wb, wo, gnf, sc2, sh2, wr, br, seq):
    t, d = x2.shape
    tiles_per_seq = seq // TM_MERGE
    row = lambda i: (i, 0)
    per_batch = lambda i: (i // tiles_per_seq, 0, 0)
    c2 = lambda i: (0, 0)
    c3 = lambda i: (0, 0, 0)
    vec = pl.BlockSpec((None, 1, d), per_batch)
    ytile = pl.BlockSpec((TM_MERGE, BRANCH_WIDTH), row)
    return pl.pallas_call(
        _merge_kernel,
        out_shape=(
            jax.ShapeDtypeStruct((t, d), F32),
            jax.ShapeDtypeStruct((t * (d // LANES), LANES), F32),
            jax.ShapeDtypeStruct((t, LANES), I32),
            jax.ShapeDtypeStruct((t, LANES), F32),
        ),
        grid=(t // TM_MERGE,),
        in_specs=[
            pl.BlockSpec((TM_MERGE, d), row), vec, vec, vec,
            pl.BlockSpec((1, d), c2),
            ytile, ytile, ytile, ytile,
            pl.BlockSpec((d, N_BRANCHES * d), c2),
            pl.BlockSpec((N_BRANCHES, BRANCH_WIDTH, d), c3),
            pl.BlockSpec((d, d), c2),
            pl.BlockSpec((1, d), c2), vec, vec,
            pl.BlockSpec((d, LANES), c2),
            pl.BlockSpec((1, LANES), c2),
        ],
        out_specs=(
            pl.BlockSpec((TM_MERGE, d), row),
            pl.BlockSpec((TM_MERGE * (d // LANES), LANES), row),
            pl.BlockSpec((TM_MERGE, LANES), row),
            pl.BlockSpec((TM_MERGE, LANES), row),
        ),
        compiler_params=_params("arbitrary"),
        name="merge_router",
    )(x2, sc1, sh1, gt1, gnm, *ys, wm, wb, wo, gnf, sc2, sh2, wr, br)


def _route(top_i, top_w, n_tok, n_tiles):
    tm = TM_EXPERT
    flat_e = top_i.reshape(-1)
    flat_w = top_w.reshape(-1)
    n_assign = flat_e.shape[0]
    sorted_e, order, sorted_w = lax.sort((flat_e, jnp.arange(n_assign, dtype=I32), flat_w), num_keys=1, is_stable=True)
    counts = jnp.sum((flat_e[:, None] == jnp.arange(N_EXPERTS, dtype=I32)[None, :]).astype(I32), axis=0)
    bounds = jnp.concatenate([jnp.zeros((1,), I32), jnp.cumsum(counts).astype(I32)])
    padded = ((counts + tm - 1) // tm) * tm
    gend = jnp.cumsum(padded)
    gstart = gend - padded
    tile_start = jnp.arange(n_tiles, dtype=I32) * tm
    tile_e_raw = jnp.sum((tile_start[:, None] >= gend[None, :]).astype(I32), axis=1)
    tile_e = jnp.minimum(tile_e_raw, N_EXPERTS - 1).astype(I32)
    n_used = (gend[-1] // tm).astype(I32).reshape(1)
    t_off = tile_start - gstart[tile_e]
    t_cnt = jnp.where(tile_e_raw < N_EXPERTS, counts[tile_e], 0)
    row = jnp.arange(tm, dtype=I32)[None, :]
    j = t_off[:, None] + row
    valid = j < t_cnt[:, None]
    p = jnp.clip(bounds[tile_e][:, None] + j, 0, n_assign - 1)
    a = order[p]
    tok = a // TOP_K
    kk = a % TOP_K
    src = jnp.where(valid, tok, 0).astype(I32)
    dst = jnp.where(valid, kk * n_tok + tok, TOP_K * n_tok + row).astype(I32)
    w = jnp.where(valid, sorted_w[p], 0.0).astype(F32)
    tile_nv = jnp.clip(t_cnt - t_off, 0, tm).astype(I32)
    return tile_e, n_used, tile_nv, src.reshape(-1), dst.reshape(-1), w.reshape(-1, 1)


def _moe_kernel(te_ref, nu_ref, nv_ref, src_hbm, dst_hbm, w_ref, h_hbm, wgu_ref, bgu_ref, wd_ref, bd_ref,
                y_hbm, src_s, dst_s, xs, ys, wgu_b, wd_b, isem, gsem, ssem):
    i = pl.program_id(0)
    tm = TM_EXPERT
    sub = xs.shape[1] // tm
    d = sub * LANES
    n_used = nu_ref[0]

    def tok_rows(t):
        return pl.ds(pl.multiple_of(t * sub, sub), sub)

    def idx_copies(tile, buf):
        base = pl.multiple_of(tile * tm, tm)
        return (pltpu.make_async_copy(src_hbm.at[pl.ds(base, tm)], src_s.at[buf], isem.at[buf, 0]),
                pltpu.make_async_copy(dst_hbm.at[pl.ds(base, tm)], dst_s.at[buf], isem.at[buf, 1]))

    def start_idx(tile, buf):
        for cp in idx_copies(tile, buf):
            cp.start()

    def wait_idx(tile, buf):
        for cp in idx_copies(tile, buf):
            cp.wait()

    def groups(tile):
        return lax.shift_right_logical(nv_ref[tile] + (ROW_GROUP - 1), ROW_GROUP.bit_length() - 1)

    def moved_rows(tile):
        return pl.multiple_of(groups(tile) * (ROW_GROUP * sub), ROW_GROUP * sub)

    def issue_gather(tile, buf, slot):
        def body(g, carry):
            for u in range(ROW_GROUP):
                r = g * ROW_GROUP + u
                pltpu.make_async_copy(h_hbm.at[tok_rows(src_s[buf, r])], xs.at[slot, tok_rows(r)], gsem.at[slot]).start()
            return carry
        lax.fori_loop(0, groups(tile), body, 0)

    def wait_gather(tile, slot):
        n = moved_rows(tile)
        pltpu.make_async_copy(h_hbm.at[pl.ds(0, n)], xs.at[slot, pl.ds(0, n)], gsem.at[slot]).wait()

    def wait_scatter(tile):
        n = moved_rows(tile)
        pltpu.make_async_copy(ys.at[pl.ds(0, n)], y_hbm.at[pl.ds(0, n)], ssem.at[0]).wait()

    @pl.when(i == 0)
    def _():
        ys[...] = jnp.zeros(ys.shape, F32)
        fill = pltpu.make_async_copy(ys, y_hbm.at[pl.ds(y_hbm.shape[0] - tm * sub, tm * sub)], ssem.at[0])
        fill.start()
        fill.wait()
        xs[...] = jnp.zeros(xs.shape, F32)

        @pl.when(n_used > 0)
        def _():
            start_idx(0, 0)
            wait_idx(0, 0)
            issue_gather(0, 0, 0)

        @pl.when(n_used > 1)
        def _():
            start_idx(1, 1)

    @pl.when(i < n_used)
    def _():
        slot = lax.rem(i, 2)
        buf = lax.rem(i, 3)

        @pl.when(i + 1 < n_used)
        def _():
            nbuf = lax.rem(i + 1, 3)
            wait_idx(i + 1, nbuf)
            issue_gather(i + 1, nbuf, 1 - slot)

        @pl.when(i + 2 < n_used)
        def _():
            start_idx(i + 2, lax.rem(i + 2, 3))

        @pl.when((i == 0) | (te_ref[i] != te_ref[jnp.maximum(i - 1, 0)]))
        def _():
            rows = d // 8

            def cast(c, carry):
                r = pl.ds(pl.multiple_of(c * rows, rows), rows)
                wgu_b[r, :] = wgu_ref[r, :].astype(BF16)
                wd_b[r, :] = wd_ref[r, :].astype(BF16)
                return carry
            lax.fori_loop(0, 8, cast, 0)

        wait_gather(i, slot)
        x = jnp.concatenate([xs[slot, pl.ds(c, tm, stride=sub), :] for c in range(sub)], axis=1).astype(BF16)
        gu = _dot(x, wgu_b[...]) + bgu_ref[...]
        x_glu = jnp.minimum(gu[:, :d], SWIGLU_LIMIT)
        x_lin = jnp.clip(gu[:, d:], -SWIGLU_LIMIT, SWIGLU_LIMIT)
        act = x_glu * jax.nn.sigmoid(SWIGLU_ALPHA * x_glu) * (x_lin + 1.0)
        out = (_dot(act.astype(BF16), wd_b[...]) + bd_ref[...]) * w_ref[...]

        @pl.when(i > 0)
        def _():
            wait_scatter(i - 1)

        for c in range(sub):
            ys[pl.ds(c, tm, stride=sub), :] = out[:, c * LANES:(c + 1) * LANES]

        def scatter(g, carry):
            for u in range(ROW_GROUP):
                r = g * ROW_GROUP + u
                pltpu.make_async_copy(ys.at[tok_rows(r)], y_hbm.at[tok_rows(dst_s[buf, r])], ssem.at[0]).start()
            return carry
        lax.fori_loop(0, groups(i), scatter, 0)

        @pl.when(i == n_used - 1)
        def _():
            wait_scatter(i)


def _moe(h2_rows, tile_e, n_used, tile_nv, src, dst, w_slot, wgu, bgu, wd, bd, layer):
    d = wgu.shape[2]
    sub = d // LANES
    t = h2_rows.shape[0] // sub
    n_exp = wgu.shape[1]
    n_tiles = tile_e.shape[0]
    tm = TM_EXPERT
    any_spec = pl.BlockSpec(memory_space=pl.ANY)
    expert = lambda i, te, nu, nv: (layer, te[i], 0, 0)
    grid_spec = pltpu.PrefetchScalarGridSpec(
        num_scalar_prefetch=3,
        grid=(n_tiles,),
        in_specs=[
            any_spec, any_spec,
            pl.BlockSpec((tm, 1), lambda i, te, nu, nv: (i, 0)),
            any_spec,
            pl.BlockSpec((None, None, d, 2 * d), expert),
            pl.BlockSpec((None, None, 1, 2 * d), expert),
            pl.BlockSpec((None, None, d, d), expert),
            pl.BlockSpec((None, None, 1, d), expert),
        ],
        out_specs=any_spec,
        scratch_shapes=[
            pltpu.SMEM((3, tm), I32),
            pltpu.SMEM((3, tm), I32),
            pltpu.VMEM((2, tm * sub, LANES), F32),
            pltpu.VMEM((tm * sub, LANES), F32),
            pltpu.VMEM((d, 2 * d), BF16),
            pltpu.VMEM((d, d), BF16),
            pltpu.SemaphoreType.DMA((3, 2)),
            pltpu.SemaphoreType.DMA((2,)),
            pltpu.SemaphoreType.DMA((1,)),
        ],
    )
    return pl.pallas_call(
        _moe_kernel,
        out_shape=jax.ShapeDtypeStruct(((TOP_K * t + tm) * sub, LANES), F32),
        grid_spec=grid_spec,
        compiler_params=pltpu.CompilerParams(dimension_semantics=("arbitrary",), has_side_effects=True),
        name="moe_grouped",
    )(tile_e, n_used, tile_nv, src, dst, w_slot, h2_rows, wgu, bgu.reshape(-1, n_exp, 1, 2 * d), wd, bd.reshape(-1, n_exp, 1, d))


def _combine_kernel(x_ref, gt_ref, y0_ref, y1_ref, y2_ref, y3_ref, gf_ref, o_ref, y_sc, *, final):
    tm, d = x_ref.shape
    sub = d // LANES
    y = (y0_ref[...] + y1_ref[...]) + (y2_ref[...] + y3_ref[...])
    y_sc[...] = y
    y = jnp.concatenate([y_sc[pl.ds(c, tm, stride=sub), :] for c in range(sub)], axis=1)
    x = x_ref[...] + gt_ref[...] * y
    if final:
        ms = jnp.mean(x * x, axis=-1, keepdims=True)
        x = x * lax.rsqrt(ms + NORM_EPS) * gf_ref[...]
    o_ref[...] = x


def _combine(x1, gate2, planes, g_final, seq, final):
    t, d = x1.shape
    tiles_per_seq = seq // TM_COMBINE
    nt = t // TM_COMBINE
    row = lambda i: (i, 0)
    plane = lambda k: (lambda i: (k * nt + i, 0))
    blk = (TM_COMBINE, d)
    pblk = (TM_COMBINE * (d // LANES), LANES)
    return pl.pallas_call(
        functools.partial(_combine_kernel, final=final),
        out_shape=jax.ShapeDtypeStruct((t, d), F32),
        grid=(nt,),
        in_specs=[
            pl.BlockSpec(blk, row),
            pl.BlockSpec((None, 1, d), lambda i: (i // tiles_per_seq, 0, 0)),
            pl.BlockSpec(pblk, plane(0)), pl.BlockSpec(pblk, plane(1)),
            pl.BlockSpec(pblk, plane(2)), pl.BlockSpec(pblk, plane(3)),
            pl.BlockSpec((1, d), lambda i: (0, 0)),
        ],
        out_specs=pl.BlockSpec(blk, row),
        scratch_shapes=[pltpu.VMEM(pblk, F32)],
        compiler_params=_params("arbitrary"),
        name="moe_combine",
    )(x1, gate2, planes, planes, planes, planes, g_final)


def _pack_w_in(w):
    sizes = (256, 256, 256, 128, 128, 256, GLA_GATE_RANK, 256, 256, 256, 256, 256, 256, 128, 128)
    offs = np.concatenate([[0], np.cumsum(sizes)])
    mq, mk, mv, gq, gk, gv, ga, gr, rq, rk, rv, rg, sq, sk, sv = (w[:, int(offs[n]):int(offs[n + 1])] for n in range(15))
    merge = w[:, int(offs[15]):]
    d = w.shape[0]
    rep = lambda kv: jnp.repeat(kv.reshape(d, SWA_KV_HEADS, HEAD_DIM), BRANCH_HEADS // SWA_KV_HEADS, axis=1).reshape(d, BRANCH_WIDTH)
    scale = HEAD_DIM ** -0.5
    ga_pad = jnp.pad(ga, ((0, 0), (0, LANES - GLA_GATE_RANK)))
    packed = jnp.concatenate([mq * scale, mk, mv, gq, gk, gv, ga_pad, gr, rq, rk * scale, rv, rg,
                              sq * scale, rep(sk), rep(sv)], axis=1)
    return packed.astype(BF16), merge.astype(BF16)


def kernel(x, c, w_ada, b_ada, g_norm_mix, w_in, w_gla_gate, b_gla_gate, g_gla_norm, g_ret_norm, attn_sinks,
           w_branch, w_out, g_norm_ffn, w_router, b_router, w_gate_up, b_gate_up, w_down, b_down, g_final):
    batch, seq, d = x.shape
    depth = w_ada.shape[0]
    t = batch * seq
    n_alibi = 2 * BRANCH_HEADS
    slopes = [2.0 ** (-(k + 1.0) * (8.0 / n_alibi)) for k in range(n_alibi)]
    swa_slopes, moba_slopes = tuple(slopes[:BRANCH_HEADS]), tuple(slopes[BRANCH_HEADS:])
    log_gamma = jnp.log(1.0 - 2.0 ** (-RET_DECAY_BASE - jnp.arange(BRANCH_HEADS, dtype=F32)))
    log_gamma_row = jnp.repeat(log_gamma, HEAD_DIM).reshape(1, BRANCH_WIDTH)
    dummy_row = jnp.zeros((1, BRANCH_HEADS * GLA_KEY_DIM), F32)
    n_tiles = (TOP_K * t + N_EXPERTS * (TM_EXPERT - 1) + TM_EXPERT - 1) // TM_EXPERT

    mod = _modulation(c, w_ada, b_ada)
    xf = x.reshape(t, d)
    for l in range(depth):
        shift1, scale1, gate1, shift2, scale2, gate2 = (
            mod[l, :, n * d:(n + 1) * d].reshape(batch, 1, d) for n in range(6))
        w_packed, w_merge = _pack_w_in(w_in[l])
        wg = jnp.pad(w_gla_gate[l], ((0, LANES - GLA_GATE_RANK), (0, 0)))
        moba_qkv, gla_f, gla_v, ret_f, ret_v, swa_qkv, kmean = _input_projection(
            xf, scale1, shift1, g_norm_mix[l].reshape(1, d), w_packed, wg, b_gla_gate[l].reshape(1, -1), seq)
        kmean = kmean.reshape(batch, seq // MOBA_BLOCK, BRANCH_WIDTH)

        y_moba = _moba(moba_qkv, kmean, batch, seq, moba_slopes)
        y_gla = _linear_attention(gla_f, gla_v, g_gla_norm[l].reshape(1, -1), dummy_row, batch, seq,
                                  dk=GLA_KEY_DIM, gated=True, qoff=0, koff=128, laoff=256, roff=384,
                                  norm="rms", name="gla")
        y_ret = _linear_attention(ret_f, ret_v, g_ret_norm[l].reshape(1, -1), log_gamma_row, batch, seq,
                                  dk=HEAD_DIM, gated=False, qoff=0, koff=256, laoff=0, roff=512,
                                  norm="group", name="retention")
        sinks_row = jnp.pad(attn_sinks[l].reshape(1, -1), ((0, 0), (0, LANES - BRANCH_HEADS)))
        y_swa = _swa(swa_qkv, sinks_row, batch, seq, swa_slopes)

        wr = jnp.pad(w_router[l], ((0, 0), (0, LANES - N_EXPERTS)))
        br = jnp.pad(b_router[l].reshape(1, -1), ((0, 0), (0, LANES - N_EXPERTS)), constant_values=NEG_BIG)
        x1, h2, top_i, top_w = _merge(
            xf, scale1, shift1, gate1, g_norm_mix[l].reshape(1, d), (y_moba, y_gla, y_ret, y_swa),
            w_merge, w_branch[l].astype(BF16), w_out[l].astype(BF16),
            g_norm_ffn[l].reshape(1, d), scale2, shift2, wr, br, seq)

        tile_e, n_used, tile_nv, src, dst, w_slot = _route(top_i[:, :TOP_K], top_w[:, :TOP_K], t, n_tiles)
        planes = _moe(h2, tile_e, n_used, tile_nv, src, dst, w_slot, w_gate_up, b_gate_up, w_down, b_down, l)
        xf = _combine(x1, gate2, planes, g_final.reshape(1, d), seq, final=(l == depth - 1))
    return xf.reshape(batch, seq, d)
```
